```python
import jax
import jax.numpy as jnp
from jax import lax
import numpy as np

D_MODEL = 1024
BATCH = 8
SEQ = 2048
DEPTH = 4
DEC_BATCH = 32
DEC_SEQ = 4
PAST_LEN = 8192
PAGE_SIZE = 128

HEAD_DIM = 64
N_MIX_HEADS = 12
MIX_WIDTH = N_MIX_HEADS * HEAD_DIM
N_MEM_HEADS = 4
MEM_WIDTH = N_MEM_HEADS * HEAD_DIM
N_MEM = 256
MOBA_BLOCK = 256
MOBA_TOPK = 3
MOBA_QCHUNK = 16
SWA_GROUPS = ((128, 1), (512, 4), (2048, 16))
SWA_HEADS = N_MIX_HEADS // len(SWA_GROUPS)
SWA_OUT_WIDTH = SWA_HEADS * HEAD_DIM
CONV_CH = MIX_WIDTH
CONV_WIDTH = 31
D_FF = 4 * D_MODEL
N_MIXERS = 3
LAYER_KINDS = tuple(i % N_MIXERS for i in range(DEPTH))
N_LAYERS_A = LAYER_KINDS.count(0)
N_LAYERS_B = LAYER_KINDS.count(1)
N_LAYERS_C = LAYER_KINDS.count(2)
ATTN_SCALE = HEAD_DIM ** -0.5
NEG = -1e30
RMS_EPS = 1e-6
LN_EPS = 1e-5
F32 = jnp.float32

kernel_name = 'hybrid_moba_dilated_conformer_decoder_step'


def rmsnorm(x, g):
    xf = x.astype(F32)
    y = xf * lax.rsqrt(jnp.mean(xf * xf, axis=-1, keepdims=True) + RMS_EPS)
    return (y * g.astype(F32)).astype(x.dtype)


def layernorm(x, g, b):
    xf = x.astype(F32)
    mu = jnp.mean(xf, axis=-1, keepdims=True)
    var = jnp.mean(jnp.square(xf - mu), axis=-1, keepdims=True)
    y = (xf - mu) * lax.rsqrt(var + LN_EPS)
    return (y * g.astype(F32) + b.astype(F32)).astype(x.dtype)


def split_heads(t):
    return t.reshape(t.shape[:-1] + (-1, HEAD_DIM))


def merge_heads(t):
    return t.reshape(t.shape[:-2] + (-1,))


def sq_relu_mlp(h, w_up, w_down):
    return jnp.square(jax.nn.relu(h @ w_up)) @ w_down


def glu(z):
    a, g = jnp.split(z, 2, axis=-1)
    return a * jax.nn.sigmoid(g)


def memory_kv(mem, g, w):
    z = rmsnorm(mem, g) @ w
    return split_heads(z[..., :MEM_WIDTH]), split_heads(z[..., MEM_WIDTH:])


def memory_attend(q, mk, mv):
    s = jnp.einsum('blhd,bnhd->blhn', q, mk, preferred_element_type=F32) * ATTN_SCALE
    p = jax.nn.softmax(s, axis=-1)
    return jnp.einsum('blhn,bnhd->blhd', p.astype(mv.dtype), mv)


def moba_prompt(q, k, v):
    B, S, H, dh = q.shape
    nb = -(-S // MOBA_BLOCK)
    pad = nb * MOBA_BLOCK - S

    def blocks(t):
        t = jnp.pad(t, ((0, 0), (0, pad), (0, 0), (0, 0)))
        return t.reshape(B, nb, MOBA_BLOCK, H, dh).transpose(0, 3, 1, 2, 4)

    kb, vb = blocks(k), blocks(v)
    k_mean = jnp.mean(kb.astype(F32), axis=3)
    ksel = min(MOBA_TOPK, nb)
    nq = S // MOBA_QCHUNK
    q_chunks = q.reshape(B, nq, MOBA_QCHUNK, H, dh).transpose(1, 0, 2, 3, 4)
    b_idx = jnp.arange(B)[:, None, None, None]
    h_idx = jnp.arange(H)[None, None, :, None]
    blk_rows = jnp.arange(MOBA_BLOCK)

    def attend_chunk(args):
        qc, c = args
        qpos = c * MOBA_QCHUNK + jnp.arange(MOBA_QCHUNK)
        qblk_b = (qpos // MOBA_BLOCK)[None, :, None, None]
        gate = jnp.einsum('bqhd,bhnd->bqhn', qc.astype(F32), k_mean)
        gate = jnp.where(jnp.arange(nb) < qblk_b, gate, NEG)
        _, sel = lax.top_k(gate, ksel)
        own = jnp.broadcast_to(qblk_b, sel.shape[:-1] + (1,)).astype(sel.dtype)
        idx = jnp.concatenate([sel, own], axis=-1)
        slot_ok = jnp.concatenate([sel < qblk_b, jnp.ones(own.shape, bool)], axis=-1)
        kg = kb[b_idx, h_idx, idx]
        vg = vb[b_idx, h_idx, idx]
        kpos = idx[..., None] * MOBA_BLOCK + blk_rows
        ok = slot_ok[..., None] & (kpos <= qpos[None, :, None, None, None])
        s = jnp.einsum('bqhd,bqhjrd->bqhjr', qc, kg, preferred_element_type=F32) * ATTN_SCALE
        s = jnp.where(ok, s, NEG)
        p = jax.nn.softmax(s.reshape(s.shape[:3] + (-1,)), axis=-1).reshape(s.shape)
        return jnp.einsum('bqhjr,bqhjrd->bqhd', p.astype(vg.dtype), vg)

    out = lax.map(attend_chunk, (q_chunks, jnp.arange(nq)))
    return out.transpose(1, 0, 2, 3, 4).reshape(B, S, H, dh)


def moba_sample(q, k_new, v_new, cache_k, cache_v, layer, page_table):
    Bd, T, H, dh = q.shape
    n_pages = page_table.shape[1]
    P = n_pages * PAGE_SIZE
    ppb = MOBA_BLOCK // PAGE_SIZE
    nbc = P // MOBA_BLOCK
    own_start = nbc * MOBA_BLOCK
    qpos = P + jnp.arange(T)
    qblk = qpos // MOBA_BLOCK
    k_own, v_own = k_new, v_new
    if own_start < P:
        tail = page_table[:, own_start // PAGE_SIZE:]
        k_own = jnp.concatenate([cache_k[layer, tail].reshape(Bd, -1, H, dh), k_new], axis=1)
        v_own = jnp.concatenate([cache_v[layer, tail].reshape(Bd, -1, H, dh), v_new], axis=1)
    own_pos = own_start + jnp.arange(k_own.shape[1])
    own_ok = (own_pos[None, :] <= qpos[:, None]) & (own_pos[None, :] // MOBA_BLOCK == qblk[:, None])
    s_own = jnp.einsum('bthd,bnhd->bthn', q, k_own, preferred_element_type=F32) * ATTN_SCALE
    s_own = jnp.where(own_ok[None, :, None, :], s_own, NEG)
    if nbc == 0:
        p = jax.nn.softmax(s_own, axis=-1)
        return jnp.einsum('bthn,bnhd->bthd', p.astype(v_own.dtype), v_own)
    ksel = min(MOBA_TOPK, nbc)
    k_past = cache_k[layer, page_table[:, :nbc * ppb]]
    k_mean = jnp.mean(k_past.reshape(Bd, nbc, MOBA_BLOCK, H, dh).astype(F32), axis=2)
    gate = jnp.einsum('bthd,bnhd->bthn', q.astype(F32), k_mean)
    qblk_b = qblk[None, :, None, None]
    gate = jnp.where(jnp.arange(nbc) < qblk_b, gate, NEG)
    _, sel = lax.top_k(gate, ksel)
    sel_ok = sel < qblk_b
    b_idx = jnp.arange(Bd)[:, None, None, None, None]
    phys = page_table[b_idx, sel[..., None] * ppb + jnp.arange(ppb)]
    h_idx = jnp.arange(H)[None, None, :, None, None, None]
    rows = jnp.arange(PAGE_SIZE)
    n_sel = ksel * MOBA_BLOCK
    kg = cache_k[layer, phys[..., None], rows, h_idx].reshape(Bd, T, H, n_sel, dh)
    vg = cache_v[layer, phys[..., None], rows, h_idx].reshape(Bd, T, H, n_sel, dh)
    s_sel = jnp.einsum('bthd,bthnd->bthn', q, kg, preferred_element_type=F32) * ATTN_SCALE
    s_sel = jnp.where(jnp.repeat(sel_ok, MOBA_BLOCK, axis=-1), s_sel, NEG)
    p = jax.nn.softmax(jnp.concatenate([s_sel, s_own], axis=-1), axis=-1)
    return (jnp.einsum('bthn,bthnd->bthd', p[..., :n_sel].astype(vg.dtype), vg)
            + jnp.einsum('bthn,bnhd->bthd', p[..., n_sel:].astype(v_own.dtype), v_own))


def dilated_prompt(q, k, v, window, dil):
    B, S, H, dh = q.shape
    n = window // dil
    L = S // dil
    nblk = -(-L // n)
    Lp = nblk * n

    def to_blocks(t):
        t = t.reshape(B, L, dil, H, dh)
        t = jnp.pad(t, ((0, 0), (0, Lp - L), (0, 0), (0, 0), (0, 0)))
        return t.reshape(B, nblk, n, dil, H, dh)

    def with_prev(t):
        prev = jnp.concatenate([jnp.zeros_like(t[:, :1]), t[:, :-1]], axis=1)
        return jnp.concatenate([prev, t], axis=2)

    qb = to_blocks(q)
    kk, vv = with_prev(to_blocks(k)), with_prev(to_blocks(v))
    a = jnp.arange(n)[:, None]
    c = jnp.arange(2 * n)[None, :]
    dist = n + a - c
    band = (dist >= 0) & (dist <= n)
    has_prev = (jnp.arange(nblk) > 0)[:, None, None] | (c >= n)[None]
    mask = band[None] & has_prev
    s = jnp.einsum('bgqrhd,bgcrhd->bgrhqc', qb, kk, preferred_element_type=F32) * ATTN_SCALE
    s = jnp.where(mask[None, :, None, None], s, NEG)
    m = jnp.max(s, axis=-1, keepdims=True)
    e = jnp.exp(s - m)
    den = jnp.sum(e, axis=-1, keepdims=True)
    o = jnp.einsum('bgrhqc,bgcrhd->bgqrhd', (e / den).astype(vv.dtype), vv)
    lse = (m + jnp.log(den))[..., 0]
    o = o.reshape(B, Lp, dil, H, dh)[:, :L].reshape(B, S, H, dh)
    lse = lse.transpose(0, 1, 4, 2, 3).reshape(B, Lp, dil, H)[:, :L].reshape(B, S, H)
    return o, lse


def dilated_sample(q, k_new, v_new, buf, window, dil, past_len):
    Bd, T, H, dh = q.shape
    Lb = buf.shape[1]
    k_all = jnp.concatenate([buf[:, :, 0], k_new], axis=1)
    v_all = jnp.concatenate([buf[:, :, 1], v_new], axis=1)
    n = window // dil
    idx = Lb + jnp.arange(T)[:, None] - dil * jnp.arange(n + 1)[None, :]
    ok = idx >= 0
    idx = jnp.maximum(idx, 0)
    kg, vg = k_all[:, idx], v_all[:, idx]
    s = jnp.einsum('bthd,btjhd->bthj', q, kg, preferred_element_type=F32) * ATTN_SCALE
    s = jnp.where(ok[None, :, None, :], s, NEG)
    m = jnp.max(s, axis=-1, keepdims=True)
    e = jnp.exp(s - m)
    den = jnp.sum(e, axis=-1, keepdims=True)
    o = jnp.einsum('bthj,btjhd->bthd', (e / den).astype(vg.dtype), vg)
    lse = (m + jnp.log(den))[..., 0]
    new_buf = jnp.stack([k_all, v_all], axis=2)[:, -min(window, past_len + T):]
    return o, lse, new_buf


def combine_by_denominator(outs, lses):
    w = jax.nn.softmax(jnp.stack(lses, axis=0), axis=0)
    return jnp.sum(w[..., None].astype(outs[0].dtype) * jnp.stack(outs, axis=0), axis=0)


def conformer_conv(u_ctx, w, b, ln_g, ln_b):
    y = lax.conv_general_dilated(u_ctx, w[:, None, :].astype(u_ctx.dtype), (1,), 'VALID',
                                 dimension_numbers=('NWC', 'WIO', 'NWC'),
                                 feature_group_count=u_ctx.shape[-1])
    return jax.nn.silu(layernorm(y + b, ln_g, ln_b))


def setup_inputs(seed: int = 0) -> dict:
    key = jax.random.key(seed)
    keys = iter(jax.random.split(key, 40))

    def normal(shape, scale=1.0):
        return scale * jax.random.normal(next(keys), shape, F32)

    def gain(shape):
        return 1.0 + 0.02 * normal(shape)

    D = D_MODEL
    n_pages = PAST_LEN // PAGE_SIZE
    n_used = DEC_BATCH * n_pages
    n_pool = n_used + max(1, n_used // 4)
    x_prompt = normal((BATCH, SEQ, D))
    x_sample = normal((DEC_BATCH, DEC_SEQ, D))
    mem_prompt = normal((BATCH, N_MEM, D))
    cache_moba_k = normal((N_LAYERS_A, n_pool, PAGE_SIZE, N_MIX_HEADS, HEAD_DIM))
    cache_moba_v = normal((N_LAYERS_A, n_pool, PAGE_SIZE, N_MIX_HEADS, HEAD_DIM))
    page_table = jax.random.permutation(next(keys), n_pool)[:n_used].reshape(DEC_BATCH, n_pages).astype(jnp.int32)
    state_swa_w128 = normal((N_LAYERS_B, DEC_BATCH, min(SWA_GROUPS[0][0], PAST_LEN), 2, SWA_HEADS, HEAD_DIM))
    state_swa_w512 = normal((N_LAYERS_B, DEC_BATCH, min(SWA_GROUPS[1][0], PAST_LEN), 2, SWA_HEADS, HEAD_DIM))
    state_swa_w2048 = normal((N_LAYERS_B, DEC_BATCH, min(SWA_GROUPS[2][0], PAST_LEN), 2, SWA_HEADS, HEAD_DIM))
    state_conv = normal((N_LAYERS_C, DEC_BATCH, CONV_WIDTH - 1, CONV_CH), 0.5)
    cache_mem_k = normal((DEPTH, DEC_BATCH, N_MEM, N_MEM_HEADS, HEAD_DIM))
    cache_mem_v = normal((DEPTH, DEC_BATCH, N_MEM, N_MEM_HEADS, HEAD_DIM))
    g_mix = gain((DEPTH, D))
    g_mem = gain((DEPTH, D))
    w_mem_kv = normal((DEPTH, D, 2 * MEM_WIDTH), D ** -0.5)
    w_in_a = normal((N_LAYERS_A, D, 3 * MIX_WIDTH + MEM_WIDTH), D ** -0.5)
    w_out_a = normal((N_LAYERS_A, MIX_WIDTH + MEM_WIDTH, D), (MIX_WIDTH + MEM_WIDTH) ** -0.5)
    w_in_b = normal((N_LAYERS_B, D, 3 * MIX_WIDTH + MEM_WIDTH), D ** -0.5)
    w_out_b = normal((N_LAYERS_B, SWA_OUT_WIDTH + MEM_WIDTH, D), (SWA_OUT_WIDTH + MEM_WIDTH) ** -0.5)
    w_in_c = normal((N_LAYERS_C, D, 2 * CONV_CH + MEM_WIDTH), D ** -0.5)
    conv_w = normal((N_LAYERS_C, CONV_WIDTH, CONV_CH), CONV_WIDTH ** -0.5)
    conv_b = normal((N_LAYERS_C, CONV_CH), 0.02)
    conv_ln_g = gain((N_LAYERS_C, CONV_CH))
    conv_ln_b = normal((N_LAYERS_C, CONV_CH), 0.02)
    w_out_c = normal((N_LAYERS_C, CONV_CH + MEM_WIDTH, D), (CONV_CH + MEM_WIDTH) ** -0.5)
    g_ffn = gain((DEPTH, D))
    w_ffn_up = normal((DEPTH, D, D_FF), D ** -0.5)
    w_ffn_down = normal((DEPTH, D_FF, D), D_FF ** -0.5)
    g_final = gain((D,))
    return {'x_prompt': x_prompt, 'x_sample': x_sample, 'mem_prompt': mem_prompt,
            'cache_moba_k': cache_moba_k, 'cache_moba_v': cache_moba_v, 'page_table': page_table,
            'state_swa_w128': state_swa_w128, 'state_swa_w512': state_swa_w512,
            'state_swa_w2048': state_swa_w2048, 'state_conv': state_conv,
            'cache_mem_k': cache_mem_k, 'cache_mem_v': cache_mem_v,
            'g_mix': g_mix, 'g_mem': g_mem, 'w_mem_kv': w_mem_kv,
            'w_in_a': w_in_a, 'w_out_a': w_out_a, 'w_in_b': w_in_b, 'w_out_b': w_out_b,
            'w_in_c': w_in_c, 'conv_w': conv_w, 'conv_b': conv_b, 'conv_ln_g': conv_ln_g,
            'conv_ln_b': conv_ln_b, 'w_out_c': w_out_c,
            'g_ffn': g_ffn, 'w_ffn_up': w_ffn_up, 'w_ffn_down': w_ffn_down, 'g_final': g_final}


def reference(x_prompt, x_sample, mem_prompt, cache_moba_k, cache_moba_v, page_table,
              state_swa_w128, state_swa_w512, state_swa_w2048, state_conv,
              cache_mem_k, cache_mem_v, g_mix, g_mem, w_mem_kv,
              w_in_a, w_out_a, w_in_b, w_out_b,
              w_in_c, conv_w, conv_b, conv_ln_g, conv_ln_b, w_out_c,
              g_ffn, w_ffn_up, w_ffn_down, g_final):
    S = x_prompt.shape[1]
    past_len = page_table.shape[1] * PAGE_SIZE
    swa_states = (state_swa_w128, state_swa_w512, state_swa_w2048)
    xp, xs = x_prompt, x_sample
    moba_kp, moba_vp, moba_ks, moba_vs = [], [], [], []
    swa_p = [[] for _ in SWA_GROUPS]
    swa_s = [[] for _ in SWA_GROUPS]
    conv_p, conv_s = [], []
    mem_kp, mem_vp = [], []
    for l in range(DEPTH):
        kind = LAYER_KINDS[l]
        j = LAYER_KINDS[:l].count(kind)
        hp = rmsnorm(xp, g_mix[l])
        hs = rmsnorm(xs, g_mix[l])
        mkp, mvp = memory_kv(mem_prompt, g_mem[l], w_mem_kv[l])
        mem_kp.append(mkp)
        mem_vp.append(mvp)
        if kind == 0:
            zp, zs = hp @ w_in_a[j], hs @ w_in_a[j]
            qp, kp, vp = [split_heads(zp[..., i * MIX_WIDTH:(i + 1) * MIX_WIDTH]) for i in range(3)]
            qs, ks, vs = [split_heads(zs[..., i * MIX_WIDTH:(i + 1) * MIX_WIDTH]) for i in range(3)]
            mix_p = merge_heads(moba_prompt(qp, kp, vp))
            mix_s = merge_heads(moba_sample(qs, ks, vs, cache_moba_k, cache_moba_v, j, page_table))
            moba_kp.append(kp)
            moba_vp.append(vp)
            moba_ks.append(ks)
            moba_vs.append(vs)
            w_out = w_out_a[j]
        elif kind == 1:
            zp, zs = hp @ w_in_b[j], hs @ w_in_b[j]
            qp, kp, vp = [split_heads(zp[..., i * MIX_WIDTH:(i + 1) * MIX_WIDTH]) for i in range(3)]
            qs, ks, vs = [split_heads(zs[..., i * MIX_WIDTH:(i + 1) * MIX_WIDTH]) for i in range(3)]
            outs_p, lses_p, outs_s, lses_s = [], [], [], []
            for g, (window, dil) in enumerate(SWA_GROUPS):
                hsl = slice(g * SWA_HEADS, (g + 1) * SWA_HEADS)
                o, lse = dilated_prompt(qp[:, :, hsl], kp[:, :, hsl], vp[:, :, hsl], window, dil)
                outs_p.append(o)
                lses_p.append(lse)
                swa_p[g].append(jnp.stack([kp[:, :, hsl], vp[:, :, hsl]], axis=2)[:, S - min(window, S):])
                o, lse, buf = dilated_sample(qs[:, :, hsl], ks[:, :, hsl], vs[:, :, hsl],
                                             swa_states[g][j], window, dil, past_len)
                outs_s.append(o)
                lses_s.append(lse)
                swa_s[g].append(buf)
            mix_p = merge_heads(combine_by_denominator(outs_p, lses_p))
            mix_s = merge_heads(combine_by_denominator(outs_s, lses_s))
            w_out = w_out_b[j]
        else:
            zp, zs = hp @ w_in_c[j], hs @ w_in_c[j]
            up = glu(zp[..., :2 * CONV_CH])
            us = glu(zs[..., :2 * CONV_CH])
            ctx_p = jnp.pad(up, ((0, 0), (CONV_WIDTH - 1, 0), (0, 0)))
            ctx_s = jnp.concatenate([state_conv[j], us], axis=1)
            mix_p = conformer_conv(ctx_p, conv_w[j], conv_b[j], conv_ln_g[j], conv_ln_b[j])
            mix_s = conformer_conv(ctx_s, conv_w[j], conv_b[j], conv_ln_g[j], conv_ln_b[j])
            conv_p.append(ctx_p[:, -(CONV_WIDTH - 1):])
            conv_s.append(ctx_s[:, -(CONV_WIDTH - 1):])
            w_out = w_out_c[j]
        mem_p = merge_heads(memory_attend(split_heads(zp[..., -MEM_WIDTH:]), mkp, mvp))
        mem_s = merge_heads(memory_attend(split_heads(zs[..., -MEM_WIDTH:]), cache_mem_k[l], cache_mem_v[l]))
        xp = xp + jnp.concatenate([mix_p, mem_p], axis=-1) @ w_out
        xs = xs + jnp.concatenate([mix_s, mem_s], axis=-1) @ w_out
        xp = xp + sq_relu_mlp(rmsnorm(xp, g_ffn[l]), w_ffn_up[l], w_ffn_down[l])
        xs = xs + sq_relu_mlp(rmsnorm(xs, g_ffn[l]), w_ffn_up[l], w_ffn_down[l])
    y_prompt = rmsnorm(xp, g_final)
    y_sample = rmsnorm(xs, g_final)
    return (y_prompt, y_sample,
            jnp.stack(moba_kp), jnp.stack(moba_vp), jnp.stack(moba_ks), jnp.stack(moba_vs),
            jnp.stack(swa_p[0]), jnp.stack(swa_p[1]), jnp.stack(swa_p[2]),
            jnp.stack(swa_s[0]), jnp.stack(swa_s[1]), jnp.stack(swa_s[2]),
            jnp.stack(conv_p), jnp.stack(conv_s),
            jnp.stack(mem_kp), jnp.stack(mem_vp))
```

```python
import functools

import jax
import jax.numpy as jnp
from jax import lax
from jax.experimental import pallas as pl
from jax.experimental.pallas import tpu as pltpu

F32 = jnp.float32
BF16 = jnp.bfloat16

HEAD_DIM = 64
N_MIX_HEADS = 12
MIX_WIDTH = N_MIX_HEADS * HEAD_DIM
N_MEM_HEADS = 4
MEM_WIDTH = N_MEM_HEADS * HEAD_DIM
MOBA_BLOCK = 256
MOBA_TOPK = 3
PAGE_SIZE = 128
SWA_GROUPS = ((128, 1), (512, 4), (2048, 16))
SWA_HEADS = 4
SWA_WIDTH = SWA_HEADS * HEAD_DIM
CONV_WIDTH = 31
CONV_HALO = 32
N_MIXERS = 3
ATTN_SCALE = HEAD_DIM ** -0.5
NEG = -1e30
RMS_EPS = 1e-6
LN_EPS = 1e-5

LANES = 128
VMEM_LIMIT = 48 * 1024 * 1024
NT_DIMS = (((1,), (1,)), ((), ()))
HIGHEST = lax.Precision.HIGHEST


def _cparams(n_grid):
    return pltpu.CompilerParams(dimension_semantics=("arbitrary",) * n_grid,
                                vmem_limit_bytes=VMEM_LIMIT)


def _rms(x, g):
    return x * lax.rsqrt(jnp.mean(x * x, axis=-1, keepdims=True) + RMS_EPS) * g


def _dot(a, b, precision=None):
    return jnp.dot(a, b, precision=precision, preferred_element_type=F32)


def _dot_nt(a, b, precision=None):
    return lax.dot_general(a, b, NT_DIMS, precision=precision, preferred_element_type=F32)


def _sigmoid(x):
    return 1.0 / (1.0 + jnp.exp(-x))


def _iota(shape, axis):
    return lax.broadcasted_iota(jnp.int32, shape, axis)


def _in_proj_kernel(x_ref, g_ref, w_ref, *out_refs, splits):
    h = _rms(x_ref[...], g_ref[...]).astype(BF16)
    off = 0
    for o_ref, n in zip(out_refs, splits):
        o_ref[...] = _dot(h, w_ref[:, off:off + n])
        off += n


def _in_proj(x, g, w_bf16, splits, bm):
    m, d = x.shape
    n = w_bf16.shape[1]
    assert sum(splits) == n and m % bm == 0
    return pl.pallas_call(
        functools.partial(_in_proj_kernel, splits=splits),
        grid=(m // bm,),
        in_specs=[pl.BlockSpec((bm, d), lambda i: (i, 0)),
                  pl.BlockSpec((1, d), lambda i: (0, 0)),
                  pl.BlockSpec((d, n), lambda i: (0, 0), pipeline_mode=pl.Buffered(1))],
        out_specs=[pl.BlockSpec((bm, s), lambda i: (i, 0)) for s in splits],
        out_shape=[jax.ShapeDtypeStruct((m, s), F32) for s in splits],
        compiler_params=_cparams(1),
        name="in_proj",
    )(x, g.reshape(1, d), w_bf16)


def _out_ffn_kernel(x_ref, mix_ref, mem_ref, wo_mix_ref, wo_mem_ref, g_ref, wup_ref, wdn_ref,
                    gfin_ref, o_ref, *, ff_chunk, final):
    x = (x_ref[...] + _dot(mix_ref[...].astype(BF16), wo_mix_ref[...])
         + _dot(mem_ref[...].astype(BF16), wo_mem_ref[...]))
    h = _rms(x, g_ref[...]).astype(BF16)
    o_ref[...] = x
    d_ff = wup_ref.shape[1]
    for c in range(d_ff // ff_chunk):
        u = _dot(h, wup_ref[:, c * ff_chunk:(c + 1) * ff_chunk])
        a = jnp.square(jnp.maximum(u, 0.0)).astype(BF16)
        o_ref[...] += _dot(a, wdn_ref[c * ff_chunk:(c + 1) * ff_chunk, :])
    if final:
        o_ref[...] = _rms(o_ref[...], gfin_ref[...])


def _out_ffn(x, mix, mem, wo_bf16, g, wup_bf16, wdn_bf16, g_final, final, bm):
    m, d = x.shape
    wmix = mix.shape[1]
    wmem = mem.shape[1]
    d_ff = wup_bf16.shape[1]
    const = lambda i: (0, 0)
    single = pl.Buffered(1)
    return pl.pallas_call(
        functools.partial(_out_ffn_kernel, ff_chunk=1024, final=final),
        grid=(m // bm,),
        in_specs=[pl.BlockSpec((bm, d), lambda i: (i, 0)),
                  pl.BlockSpec((bm, wmix), lambda i: (i, 0)),
                  pl.BlockSpec((bm, wmem), lambda i: (i, 0)),
                  pl.BlockSpec((wmix, d), const, pipeline_mode=single),
                  pl.BlockSpec((wmem, d), const, pipeline_mode=single),
                  pl.BlockSpec((1, d), const),
                  pl.BlockSpec((d, d_ff), const, pipeline_mode=single),
                  pl.BlockSpec((d_ff, d), const, pipeline_mode=single),
                  pl.BlockSpec((1, d), const)],
        out_specs=pl.BlockSpec((bm, d), lambda i: (i, 0)),
        out_shape=jax.ShapeDtypeStruct((m, d), F32),
        compiler_params=_cparams(1),
        name="out_ffn",
    )(x, mix, mem, wo_bf16[:wmix], wo_bf16[wmix:], g.reshape(1, d), wup_bf16, wdn_bf16,
      g_final.reshape(1, d))


def _mem_kv_kernel(mem_ref, g_ref, w_ref, kt_ref, vt_ref):
    h = _rms(mem_ref[0], g_ref[0]).astype(BF16)
    zt = _dot(h, w_ref[0]).T
    kt_ref[0, 0] = zt[:MEM_WIDTH]
    vt_ref[0, 0] = zt[MEM_WIDTH:]


def _mem_kv(mem, g_mem, w_bf16):
    b, n_mem, d = mem.shape
    depth = w_bf16.shape[0]
    out = jax.ShapeDtypeStruct((depth, b, MEM_WIDTH, n_mem), F32)
    return pl.pallas_call(
        _mem_kv_kernel,
        grid=(depth, b),
        in_specs=[pl.BlockSpec((1, n_mem, d), lambda l, i: (i, 0, 0)),
                  pl.BlockSpec((1, 1, d), lambda l, i: (l, 0, 0)),
                  pl.BlockSpec((1, d, 2 * MEM_WIDTH), lambda l, i: (l, 0, 0))],
        out_specs=[pl.BlockSpec((1, 1, MEM_WIDTH, n_mem), lambda l, i: (l, i, 0, 0))] * 2,
        out_shape=[out, out],
        compiler_params=_cparams(2),
        name="mem_kv",
    )(mem, g_mem.reshape(depth, 1, d), w_bf16)


def _mem_attn_kernel(q_ref, kt_ref, vt_ref, o_ref):
    q = q_ref[0]
    kt = kt_ref[...].astype(BF16)
    vt = vt_ref[...].astype(BF16)
    lane_head = _iota(q.shape, 1) // HEAD_DIM
    out = jnp.zeros(q.shape, F32)
    for h in range(N_MEM_HEADS):
        qh = jnp.where(lane_head == h, q * ATTN_SCALE, 0.0).astype(BF16)
        s = _dot(qh, kt)
        p = jnp.exp(s - jnp.max(s, axis=-1, keepdims=True))
        l = jnp.sum(p, axis=-1, keepdims=True)
        o = _dot_nt(p.astype(BF16), vt)
        out = jnp.where(lane_head == h, o / l, out)
    o_ref[0] = out


def _mem_attn(q3, kt_all, vt_all, layer, tq):
    nb, s, w = q3.shape
    n_mem = kt_all.shape[-1]
    kv_spec = pl.BlockSpec((None, None, w, n_mem), lambda b, i: (layer, b, 0, 0))
    return pl.pallas_call(
        _mem_attn_kernel,
        grid=(nb, s // tq),
        in_specs=[pl.BlockSpec((1, tq, w), lambda b, i: (b, i, 0)), kv_spec, kv_spec],
        out_specs=pl.BlockSpec((1, tq, w), lambda b, i: (b, i, 0)),
        out_shape=jax.ShapeDtypeStruct((nb, s, w), F32),
        compiler_params=_cparams(2),
        name="mem_attn",
    )(q3, kt_all, vt_all)


def _flash_step(carry, qb, kb, vb, ok):
    m, l, acc = carry
    s = jnp.where(ok, _dot_nt(qb, kb), NEG)
    m_new = jnp.maximum(m, jnp.max(s, axis=1, keepdims=True))
    alpha = jnp.exp(m - m_new)
    p = jnp.exp(s - m_new)
    l = alpha * l + jnp.sum(p, axis=1, keepdims=True)
    acc = alpha * acc + _dot(p.astype(BF16), vb)
    return m_new, l, acc


def _flash_init(tq):
    return (jnp.full((tq, 1), NEG, F32), jnp.zeros((tq, 1), F32), jnp.zeros((tq, LANES), F32))


def _moba_prompt_kernel(q_ref, k_ref, v_ref, o_ref, kmean_ref, kb_ref, vb_ref, selb_ref):
    qi = pl.program_id(2)
    blk = MOBA_BLOCK
    nb = k_ref.shape[1] // blk

    @pl.when(qi == 0)
    def _():
        for n in range(nb):
            kmean_ref[n:n + 1, :] = jnp.mean(k_ref[0, n * blk:(n + 1) * blk, :], axis=0,
                                             keepdims=True)
        kb_ref[...] = k_ref[0].astype(BF16)
        vb_ref[...] = v_ref[0].astype(BF16)

    q = q_ref[0]
    lane = _iota((blk, LANES), 1)
    lane_nb = _iota((nb, LANES), 1)
    row = _iota((blk, blk), 0)
    col = _iota((blk, blk), 1)
    blk_id = _iota((nb, blk), 0)
    eye = jnp.where(row == col, 1.0, 0.0).astype(BF16)
    kmean = kmean_ref[...]
    valid = blk_id < qi
    off_d = pl.multiple_of(qi * blk, blk)
    outs = []
    for e in range(2):
        qe = jnp.where(lane // HEAD_DIM == e, q, 0.0)
        kme = jnp.where(lane_nb // HEAD_DIM == e, kmean, 0.0)
        gate = jnp.where(valid, _dot_nt(kme, qe, precision=HIGHEST), NEG)
        cnt = jnp.zeros((nb, blk), F32)
        for m in range(nb):
            gm = gate[m:m + 1, :]
            beats = (gm > gate) | ((gm == gate) & (blk_id > m))
            cnt = cnt + jnp.where(beats, 1.0, 0.0)
        sel = jnp.where((cnt < MOBA_TOPK) & valid, 1.0, 0.0)
        sel16 = jnp.concatenate([sel, jnp.zeros((16 - nb, blk), F32)], axis=0).astype(BF16)
        selc = _dot_nt(eye, sel16)
        for n in range(nb):
            selb_ref[n] = jnp.broadcast_to(selc[:, n:n + 1], (blk, LANES))
        qb = (qe * ATTN_SCALE).astype(BF16)
        carry = _flash_step(_flash_init(blk), qb, kb_ref[pl.ds(off_d, blk), :],
                            vb_ref[pl.ds(off_d, blk), :], row >= col)

        def body(n, carry, qb=qb):
            off = pl.multiple_of(n * blk, blk)
            okc = selb_ref[n]
            ok = jnp.concatenate([okc, okc], axis=1) > 0.5
            return _flash_step(carry, qb, kb_ref[pl.ds(off, blk), :], vb_ref[pl.ds(off, blk), :], ok)

        m, l, acc = lax.fori_loop(0, qi, body, carry)
        outs.append(acc / l)
    o_ref[0] = jnp.where(lane < HEAD_DIM, outs[0], outs[1])


def _moba_prompt(q3, k3, v3):
    b, s, w = q3.shape
    blk = MOBA_BLOCK
    nb = s // blk
    assert s % blk == 0 and nb <= 16 and w % LANES == 0
    kv_spec = pl.BlockSpec((1, s, LANES), lambda i, p, j: (i, 0, p))
    return pl.pallas_call(
        _moba_prompt_kernel,
        grid=(b, w // LANES, nb),
        in_specs=[pl.BlockSpec((1, blk, LANES), lambda i, p, j: (i, j, p)), kv_spec, kv_spec],
        out_specs=pl.BlockSpec((1, blk, LANES), lambda i, p, j: (i, j, p)),
        out_shape=jax.ShapeDtypeStruct((b, s, w), F32),
        scratch_shapes=[pltpu.VMEM((nb, LANES), F32),
                        pltpu.VMEM((s, LANES), BF16),
                        pltpu.VMEM((s, LANES), BF16),
                        pltpu.VMEM((nb, blk, LANES), F32)],
        compiler_params=_cparams(3),
        name="moba_prompt",
    )(q3, k3, v3)


def _block_diag_q(q, n_heads, rows_per_tok):
    t_len, w = q.shape
    hrow = _iota((rows_per_tok, w), 0)
    lane_head = _iota((rows_per_tok, w), 1) // HEAD_DIM
    keep = (hrow == lane_head) & (hrow < n_heads)
    parts = [jnp.where(keep, jnp.broadcast_to(q[t:t + 1, :], (rows_per_tok, w)), 0.0)
             for t in range(t_len)]
    return jnp.concatenate(parts, axis=0)


def _head_diag_rows(o, t_len, rows_per_tok):
    w = o.shape[1]
    hrow = _iota((rows_per_tok, w), 0)
    lane_head = _iota((rows_per_tok, w), 1) // HEAD_DIM
    keep = hrow == lane_head
    return [jnp.sum(jnp.where(keep, o[t * rows_per_tok:(t + 1) * rows_per_tok, :], 0.0), axis=0,
                    keepdims=True) for t in range(t_len)]


MOBA_QROWS = 16


def _moba_sample_kernel(pt_ref, q_ref, kn_ref, vn_ref, ka_ref, kb_ref, va_ref, vb_ref, o_ref,
                        qbd_ref, sc_ref, gate_ref, acc_ref, linv_ref, *, n_blocks):
    del pt_ref
    s = pl.program_id(1)
    t_len = q_ref.shape[1]
    nr = t_len * MOBA_QROWS

    @pl.when(s == 0)
    def _():
        qbd_ref[...] = _block_diag_q(q_ref[0], N_MIX_HEADS, MOBA_QROWS)

    @pl.when(s < n_blocks)
    def _():
        qbd = qbd_ref[...]
        qb = (qbd * ATTN_SCALE).astype(BF16)
        ka = ka_ref[...]
        kb = kb_ref[...]
        sc_ref[2 * s] = _dot(qb, ka.astype(BF16))
        sc_ref[2 * s + 1] = _dot(qb, kb.astype(BF16))
        g = _dot(qbd, ka + kb, precision=HIGHEST)
        gate_ref[s] = jnp.broadcast_to(jnp.sum(g, axis=1, keepdims=True) * (1.0 / MOBA_BLOCK),
                                       (nr, LANES))

    @pl.when(s == n_blocks - 1)
    def _():
        lane = _iota((nr, LANES), 1)
        gmat = jnp.full((nr, LANES), NEG, F32)
        for n in range(n_blocks):
            gmat = jnp.where(lane == n, gate_ref[n], gmat)
        cnt = jnp.zeros((nr, LANES), F32)
        for m in range(n_blocks):
            gm = jnp.broadcast_to(gmat[:, m:m + 1], (nr, LANES))
            beats = (gm > gmat) | ((gm == gmat) & (lane > m))
            cnt = cnt + jnp.where(beats, 1.0, 0.0)
        sel = jnp.where((cnt < MOBA_TOPK) & (lane < n_blocks), 1.0, 0.0)

        def block_ok(n):
            return jnp.broadcast_to(sel[:, n:n + 1], (nr, LANES)) > 0.5

        qbd = qbd_ref[...]
        trow = _iota((nr, 1), 0) // MOBA_QROWS
        s_own = []
        for u in range(t_len):
            su = jnp.sum(qbd * kn_ref[0, u:u + 1, :], axis=1, keepdims=True) * ATTN_SCALE
            s_own.append(jnp.where(trow >= u, su, NEG))

        mx = jnp.full((nr, LANES), NEG, F32)
        for n in range(n_blocks):
            ok = block_ok(n)
            for half in range(2):
                mx = jnp.maximum(mx, jnp.where(ok, sc_ref[2 * n + half], NEG))
        m = jnp.max(mx, axis=1, keepdims=True)
        for su in s_own:
            m = jnp.maximum(m, su)
        lsum = jnp.zeros((nr, LANES), F32)
        for n in range(n_blocks):
            ok = block_ok(n)
            for half in range(2):
                p = jnp.exp(jnp.where(ok, sc_ref[2 * n + half], NEG) - m)
                sc_ref[2 * n + half] = p
                lsum = lsum + p
        l = jnp.sum(lsum, axis=1, keepdims=True)
        acc = jnp.zeros(acc_ref.shape, F32)
        for u in range(t_len):
            pu = jnp.exp(s_own[u] - m)
            l = l + pu
            acc = acc + pu * vn_ref[0, u:u + 1, :]
        acc_ref[...] = acc
        linv_ref[...] = jnp.broadcast_to(1.0 / l, (nr, LANES))

    @pl.when(s >= n_blocks)
    def _():
        j = s - n_blocks
        pa = sc_ref[2 * j].astype(BF16)
        pb = sc_ref[2 * j + 1].astype(BF16)
        acc_ref[...] += (_dot_nt(pa, va_ref[...].astype(BF16))
                         + _dot_nt(pb, vb_ref[...].astype(BF16)))

    @pl.when(s == 2 * n_blocks - 1)
    def _():
        o = acc_ref[...] * linv_ref[:, 0:1]
        for t, r in enumerate(_head_diag_rows(o, t_len, MOBA_QROWS)):
            o_ref[0, t:t + 1, :] = r


def _moba_sample(q3, kn3, vn3, ckt, cvt, layer, page_table):
    bd, t_len, w = q3.shape
    n_pages = page_table.shape[1]
    pages_per_block = MOBA_BLOCK // PAGE_SIZE
    assert pages_per_block == 2 and n_pages % pages_per_block == 0 and t_len <= MOBA_BLOCK
    nb = n_pages // pages_per_block
    nr = t_len * MOBA_QROWS
    tok = pl.BlockSpec((1, t_len, w), lambda b, s, pt: (b, 0, 0))

    def page_spec(first_step, half):
        def idx(b, s, pt):
            blk = jnp.clip(s - first_step, 0, nb - 1)
            return (layer, pt[b, 2 * blk + half], 0, 0)
        return pl.BlockSpec((None, None, w, PAGE_SIZE), idx)

    grid_spec = pltpu.PrefetchScalarGridSpec(
        num_scalar_prefetch=1,
        grid=(bd, 2 * nb),
        in_specs=[tok, tok, tok, page_spec(0, 0), page_spec(0, 1), page_spec(nb, 0), page_spec(nb, 1)],
        out_specs=tok,
        scratch_shapes=[pltpu.VMEM((nr, w), F32),
                        pltpu.VMEM((n_pages, nr, PAGE_SIZE), F32),
                        pltpu.VMEM((nb, nr, LANES), F32),
                        pltpu.VMEM((nr, w), F32),
                        pltpu.VMEM((nr, LANES), F32)])
    return pl.pallas_call(
        functools.partial(_moba_sample_kernel, n_blocks=nb),
        grid_spec=grid_spec,
        out_shape=jax.ShapeDtypeStruct((bd, t_len, w), F32),
        compiler_params=_cparams(2),
        name="moba_sample",
    )(page_table, q3, kn3, vn3, ckt, ckt, cvt, cvt)


def _dil_prompt_kernel(q0_ref, q1_ref, q2_ref, k0_ref, k1_ref, k2_ref, v0_ref, v1_ref, v2_ref,
                       o_ref, kb_ref, vb_ref):
    qi = pl.program_id(2)
    tq = q0_ref.shape[1]
    q_refs = (q0_ref, q1_ref, q2_ref)

    @pl.when(qi == 0)
    def _():
        for g, (k_ref, v_ref) in enumerate(((k0_ref, v0_ref), (k1_ref, v1_ref), (k2_ref, v2_ref))):
            kb_ref[g] = k_ref[0].astype(BF16)
            vb_ref[g] = v_ref[0].astype(BF16)

    lane = _iota((tq, LANES), 1)
    dist0 = _iota((tq, tq), 0) - _iota((tq, tq), 1)
    outs = []
    for e in range(2):
        os, lses = [], []
        for g, (window, dil) in enumerate(SWA_GROUPS):
            qb = jnp.where(lane // HEAD_DIM == e, q_refs[g][0] * ATTN_SCALE, 0.0).astype(BF16)
            n_back = -(-window // tq)

            def body(i, carry, g=g, window=window, dil=dil, qb=qb):
                off = pl.multiple_of((qi - i) * tq, tq)
                d = dist0 + i * tq
                ok = (d >= 0) & (d <= window) & ((d & (dil - 1)) == 0)
                return _flash_step(carry, qb, kb_ref[g, pl.ds(off, tq), :],
                                   vb_ref[g, pl.ds(off, tq), :], ok)

            m, l, acc = lax.fori_loop(0, jnp.minimum(qi, n_back) + 1, body, _flash_init(tq))
            os.append(acc / l)
            lses.append(m + jnp.log(l))
        mx = jnp.maximum(jnp.maximum(lses[0], lses[1]), lses[2])
        ws = [jnp.exp(x - mx) for x in lses]
        tot = ws[0] + ws[1] + ws[2]
        outs.append((ws[0] * os[0] + ws[1] * os[1] + ws[2] * os[2]) / tot)
    o_ref[0] = jnp.where(lane < HEAD_DIM, outs[0], outs[1])


def _dil_prompt(q3, k3, v3, tq):
    b, s, w = q3.shape
    n_pairs = SWA_WIDTH // LANES
    assert w == len(SWA_GROUPS) * SWA_WIDTH and s % tq == 0
    assert all(d & (d - 1) == 0 for _, d in SWA_GROUPS)

    def q_spec(g):
        return pl.BlockSpec((1, tq, LANES), lambda i, p, j: (i, j, g * n_pairs + p))

    def kv_spec(g):
        return pl.BlockSpec((1, s, LANES), lambda i, p, j: (i, 0, g * n_pairs + p))

    n_g = len(SWA_GROUPS)
    return pl.pallas_call(
        _dil_prompt_kernel,
        grid=(b, n_pairs, s // tq),
        in_specs=([q_spec(g) for g in range(n_g)] + [kv_spec(g) for g in range(n_g)]
                  + [kv_spec(g) for g in range(n_g)]),
        out_specs=pl.BlockSpec((1, tq, LANES), lambda i, p, j: (i, j, p)),
        out_shape=jax.ShapeDtypeStruct((b, s, SWA_WIDTH), F32),
        scratch_shapes=[pltpu.VMEM((n_g, s, LANES), BF16), pltpu.VMEM((n_g, s, LANES), BF16)],
        compiler_params=_cparams(3),
        name="dil_prompt",
    )(q3, q3, q3, k3, k3, k3, v3, v3, v3)


SWA_QROWS = 8


def _dil_sample_kernel(q_ref, kn_ref, vn_ref, b0_ref, b1_ref, b2_ref, o_ref):
    t_len = q_ref.shape[1]
    nr = t_len * SWA_QROWS
    q = q_ref[0]
    kn = kn_ref[0]
    vn = vn_ref[0]
    trow = _iota((nr, 1), 0) // SWA_QROWS
    os, lses = [], []
    for g, ((window, dil), buf_ref) in enumerate(zip(SWA_GROUPS, (b0_ref, b1_ref, b2_ref))):
        sl = slice(g * SWA_WIDTH, (g + 1) * SWA_WIDTH)
        qbd = _block_diag_q(q[:, sl], SWA_HEADS, SWA_QROWS)
        kt = buf_ref[0, 0].astype(BF16)
        vt = buf_ref[0, 1].astype(BF16)
        s = _dot((qbd * ATTN_SCALE).astype(BF16), kt)
        d = window + (_iota((nr, window), 0) // SWA_QROWS) - _iota((nr, window), 1)
        s = jnp.where((d <= window) & ((d & (dil - 1)) == 0), s, NEG)
        s_new = []
        for u in range(t_len):
            su = jnp.sum(qbd * kn[u:u + 1, sl], axis=1, keepdims=True) * ATTN_SCALE
            du = trow - u
            s_new.append(jnp.where((du >= 0) & ((du & (dil - 1)) == 0), su, NEG))
        m = jnp.max(s, axis=1, keepdims=True)
        for su in s_new:
            m = jnp.maximum(m, su)
        p = jnp.exp(s - m)
        l = jnp.sum(p, axis=1, keepdims=True)
        o = _dot_nt(p.astype(BF16), vt)
        for u in range(t_len):
            pu = jnp.exp(s_new[u] - m)
            l = l + pu
            o = o + pu * vn[u:u + 1, sl]
        os.append(o / l)
        lses.append(m + jnp.log(l))
    mx = jnp.maximum(jnp.maximum(lses[0], lses[1]), lses[2])
    ws = [jnp.exp(x - mx) for x in lses]
    tot = ws[0] + ws[1] + ws[2]
    out = (ws[0] * os[0] + ws[1] * os[1] + ws[2] * os[2]) / tot
    for t, r in enumerate(_head_diag_rows(out, t_len, SWA_QROWS)):
        o_ref[0, t:t + 1, :] = r


def _dil_sample(q3, kn3, vn3, bufs_t, layer):
    bd, t_len, w = q3.shape
    tok = pl.BlockSpec((1, t_len, w), lambda b: (b, 0, 0))
    buf_specs = []
    for (window, _), buf in zip(SWA_GROUPS, bufs_t):
        assert buf.shape[1:] == (bd, 2, SWA_WIDTH, window)
        buf_specs.append(pl.BlockSpec((None, 1, 2, SWA_WIDTH, window),
                                      lambda b: (layer, b, 0, 0, 0)))
    return pl.pallas_call(
        _dil_sample_kernel,
        grid=(bd,),
        in_specs=[tok, tok, tok] + buf_specs,
        out_specs=pl.BlockSpec((1, t_len, SWA_WIDTH), lambda b: (b, 0, 0)),
        out_shape=jax.ShapeDtypeStruct((bd, t_len, SWA_WIDTH), F32),
        compiler_params=_cparams(1),
        name="dil_sample",
    )(q3, kn3, vn3, *bufs_t)


def _ln_swish(y, lg, lb):
    mu = jnp.mean(y, axis=-1, keepdims=True)
    yc = y - mu
    var = jnp.mean(yc * yc, axis=-1, keepdims=True)
    z = yc * lax.rsqrt(var + LN_EPS) * lg + lb
    return z * _sigmoid(z)


def _conv_prompt_kernel(a_ref, g_ref, w_ref, b_ref, lg_ref, lb_ref, o_ref, tail_ref, ctx_ref):
    qi = pl.program_id(1)
    tq = a_ref.shape[1]
    halo = CONV_HALO

    @pl.when(qi == 0)
    def _():
        ctx_ref[0:halo, :] = jnp.zeros((halo, ctx_ref.shape[1]), F32)

    ctx_ref[halo:halo + tq, :] = a_ref[0] * _sigmoid(g_ref[0])
    first = halo - (CONV_WIDTH - 1)
    acc = jnp.zeros((tq, ctx_ref.shape[1]), F32)
    for j in range(CONV_WIDTH):
        acc = acc + ctx_ref[first + j:first + j + tq, :] * w_ref[j:j + 1, :]
    o_ref[0] = _ln_swish(acc + b_ref[...], lg_ref[...], lb_ref[...])
    tail = ctx_ref[tq:tq + halo, :]
    ctx_ref[0:halo, :] = tail
    tail_ref[0] = tail


def _conv_prompt(a3, g3, w, b, lg, lb, tq):
    bsz, s, c = a3.shape
    assert s % tq == 0 and tq >= CONV_HALO
    tile = pl.BlockSpec((1, tq, c), lambda i, j: (i, j, 0))
    vec = pl.BlockSpec((1, c), lambda i, j: (0, 0))
    return pl.pallas_call(
        _conv_prompt_kernel,
        grid=(bsz, s // tq),
        in_specs=[tile, tile, pl.BlockSpec((CONV_WIDTH, c), lambda i, j: (0, 0)), vec, vec, vec],
        out_specs=[tile, pl.BlockSpec((1, CONV_HALO, c), lambda i, j: (i, 0, 0))],
        out_shape=[jax.ShapeDtypeStruct((bsz, s, c), F32),
                   jax.ShapeDtypeStruct((bsz, CONV_HALO, c), F32)],
        scratch_shapes=[pltpu.VMEM((CONV_HALO + tq, c), F32)],
        compiler_params=_cparams(2),
        name="conv_prompt",
    )(a3, g3, w, b.reshape(1, c), lg.reshape(1, c), lb.reshape(1, c))


def _conv_sample_kernel(st_ref, a_ref, g_ref, w_ref, b_ref, lg_ref, lb_ref, o_ref, ns_ref):
    n_hist = st_ref.shape[0]
    t_len = a_ref.shape[0]
    u = [a_ref[t] * _sigmoid(g_ref[t]) for t in range(t_len)]

    def ctx(i):
        return st_ref[i] if i < n_hist else u[i - n_hist]

    for t in range(t_len):
        acc = jnp.zeros(u[0].shape, F32)
        for j in range(CONV_WIDTH):
            acc = acc + ctx(t + j) * w_ref[j:j + 1, :]
        o_ref[t] = _ln_swish(acc + b_ref[...], lg_ref[...], lb_ref[...])
    for i in range(n_hist):
        ns_ref[i] = ctx(i + t_len)


def _conv_sample(state_t, a_t, g_t, w, b, lg, lb, layer):
    _, n_hist, bd, c = state_t.shape
    t_len = a_t.shape[0]
    assert n_hist == CONV_WIDTH - 1
    tok = pl.BlockSpec((t_len, bd, c), lambda i: (0, 0, 0))
    vec = pl.BlockSpec((1, c), lambda i: (0, 0))
    return pl.pallas_call(
        _conv_sample_kernel,
        grid=(1,),
        in_specs=[pl.BlockSpec((None, n_hist, bd, c), lambda i: (layer, 0, 0, 0)), tok, tok,
                  pl.BlockSpec((CONV_WIDTH, c), lambda i: (0, 0)), vec, vec, vec],
        out_specs=[tok, pl.BlockSpec((n_hist, bd, c), lambda i: (0, 0, 0))],
        out_shape=[jax.ShapeDtypeStruct((t_len, bd, c), F32),
                   jax.ShapeDtypeStruct((n_hist, bd, c), F32)],
        compiler_params=_cparams(1),
        name="conv_sample",
    )(state_t, a_t, g_t, w, b.reshape(1, c), lg.reshape(1, c), lb.reshape(1, c))


def kernel(x_prompt, x_sample, mem_prompt, cache_moba_k, cache_moba_v, page_table, state_swa_w128, state_swa_w512, state_swa_w2048, state_conv, cache_mem_k, cache_mem_v, g_mix, g_mem, w_mem_kv, w_in_a, w_out_a, w_in_b, w_out_b, w_in_c, conv_w, conv_b, conv_ln_g, conv_ln_b, w_out_c, g_ffn, w_ffn_up, w_ffn_down, g_final):
    b, s, d = x_prompt.shape
    bd, t_len, _ = x_sample.shape
    depth = g_mix.shape[0]
    kinds = tuple(i % N_MIXERS for i in range(depth))
    past_len = page_table.shape[1] * PAGE_SIZE
    n_mem = mem_prompt.shape[1]
    bm = 512
    bf = lambda w: w.astype(BF16)

    xp = x_prompt.reshape(b * s, d)
    xs = x_sample.reshape(bd * t_len, d)

    mem_kt, mem_vt = _mem_kv(mem_prompt, g_mem, bf(w_mem_kv))
    cmem_kt = cache_mem_k.transpose(0, 1, 3, 4, 2).reshape(depth, bd, MEM_WIDTH, n_mem)
    cmem_vt = cache_mem_v.transpose(0, 1, 3, 4, 2).reshape(depth, bd, MEM_WIDTH, n_mem)
    n_pool = cache_moba_k.shape[1]
    ckt = cache_moba_k.transpose(0, 1, 3, 4, 2).reshape(-1, n_pool, MIX_WIDTH, PAGE_SIZE)
    cvt = cache_moba_v.transpose(0, 1, 3, 4, 2).reshape(-1, n_pool, MIX_WIDTH, PAGE_SIZE)
    swa_states = (state_swa_w128, state_swa_w512, state_swa_w2048)
    swa_t = [st.transpose(0, 1, 3, 4, 5, 2).reshape(st.shape[0], bd, 2, SWA_WIDTH, st.shape[2])
             for st in swa_states]
    conv_t = state_conv.transpose(0, 2, 1, 3)

    moba_kp, moba_vp, moba_ks, moba_vs = [], [], [], []
    swa_p = [[] for _ in SWA_GROUPS]
    swa_s = [[] for _ in SWA_GROUPS]
    conv_p, conv_s = [], []

    for l in range(depth):
        kind = kinds[l]
        j = kinds[:l].count(kind)
        if kind == 2:
            w_in = bf(w_in_c[j])
            splits = (MIX_WIDTH, MIX_WIDTH, MEM_WIDTH)
            ap, gp, mqp = _in_proj(xp, g_mix[l], w_in, splits, bm)
            a_s, g_s, mqs = _in_proj(xs, g_mix[l], w_in, splits, bd * t_len)
            mix_p, tail = _conv_prompt(ap.reshape(b, s, -1), gp.reshape(b, s, -1), conv_w[j],
                                       conv_b[j], conv_ln_g[j], conv_ln_b[j], 256)
            mix_p = mix_p.reshape(b * s, -1)
            tb = lambda z: z.reshape(bd, t_len, -1).transpose(1, 0, 2)
            mix_s, new_state = _conv_sample(conv_t, tb(a_s), tb(g_s), conv_w[j], conv_b[j],
                                            conv_ln_g[j], conv_ln_b[j], j)
            mix_s = mix_s.transpose(1, 0, 2).reshape(bd * t_len, -1)
            conv_p.append(tail[:, CONV_HALO - (CONV_WIDTH - 1):])
            conv_s.append(new_state.transpose(1, 0, 2))
            w_out = bf(w_out_c[j])
        else:
            w_in = bf(w_in_a[j] if kind == 0 else w_in_b[j])
            splits = (MIX_WIDTH, MIX_WIDTH, MIX_WIDTH, MEM_WIDTH)
            qp, kp, vp, mqp = _in_proj(xp, g_mix[l], w_in, splits, bm)
            qs, ks, vs, mqs = _in_proj(xs, g_mix[l], w_in, splits, bd * t_len)
            qp3, kp3, vp3 = (z.reshape(b, s, MIX_WIDTH) for z in (qp, kp, vp))
            qs3, ks3, vs3 = (z.reshape(bd, t_len, MIX_WIDTH) for z in (qs, ks, vs))
            if kind == 0:
                mix_p = _moba_prompt(qp3, kp3, vp3)
                mix_s = _moba_sample(qs3, ks3, vs3, ckt, cvt, j, page_table)
                moba_kp.append(kp.reshape(b, s, N_MIX_HEADS, HEAD_DIM))
                moba_vp.append(vp.reshape(b, s, N_MIX_HEADS, HEAD_DIM))
                moba_ks.append(ks.reshape(bd, t_len, N_MIX_HEADS, HEAD_DIM))
                moba_vs.append(vs.reshape(bd, t_len, N_MIX_HEADS, HEAD_DIM))
                w_out = bf(w_out_a[j])
            else:
                mix_p = _dil_prompt(qp3, kp3, vp3, 256)
                mix_s = _dil_sample(qs3, ks3, vs3, swa_t, j)
                for g, (window, _) in enumerate(SWA_GROUPS):
                    sl = slice(g * SWA_WIDTH, (g + 1) * SWA_WIDTH)
                    heads = lambda z: z.reshape(z.shape[0], z.shape[1], SWA_HEADS, HEAD_DIM)
                    kv_p = jnp.stack([heads(kp3[:, :, sl]), heads(vp3[:, :, sl])], axis=2)
                    swa_p[g].append(kv_p[:, s - min(window, s):])
                    kv_s = jnp.stack([heads(ks3[:, :, sl]), heads(vs3[:, :, sl])], axis=2)
                    assert swa_states[g].shape[2] == window
                    full = jnp.concatenate([swa_states[g][j], kv_s], axis=1)
                    swa_s[g].append(full[:, full.shape[1] - min(window, past_len + t_len):])
                w_out = bf(w_out_b[j])
            mix_p = mix_p.reshape(b * s, -1)
            mix_s = mix_s.reshape(bd * t_len, -1)
        mem_p = _mem_attn(mqp.reshape(b, s, MEM_WIDTH), mem_kt, mem_vt, l, 512)
        mem_s = _mem_attn(mqs.reshape(bd, t_len, MEM_WIDTH), cmem_kt, cmem_vt, l, t_len)
        final = l == depth - 1
        wup, wdn = bf(w_ffn_up[l]), bf(w_ffn_down[l])
        xp = _out_ffn(xp, mix_p, mem_p.reshape(b * s, MEM_WIDTH), w_out, g_ffn[l], wup, wdn,
                      g_final, final, bm)
        xs = _out_ffn(xs, mix_s, mem_s.reshape(bd * t_len, MEM_WIDTH), w_out, g_ffn[l], wup, wdn,
                      g_final, final, bd * t_len)

    heads_t = lambda zt: zt.reshape(depth, b, N_MEM_HEADS, HEAD_DIM, n_mem).transpose(0, 1, 4, 2, 3)
    return (xp.reshape(b, s, d), xs.reshape(bd, t_len, d),
            jnp.stack(moba_kp), jnp.stack(moba_vp), jnp.stack(moba_ks), jnp.stack(moba_vs),
            jnp.stack(swa_p[0]), jnp.stack(swa_p[1]), jnp.stack(swa_p[2]),
            jnp.stack(swa_s[0]), jnp.stack(swa_s[1]), jnp.stack(swa_s[2]),
            jnp.stack(conv_p), jnp.stack(conv_s),
            heads_t(mem_kt), heads_t(mem_vt))
```

```python
import functools

import jax
import jax.numpy as jnp
from jax import lax
from jax.experimental import pallas as pl
from jax.experimental.pallas import tpu as pltpu

F32 = jnp.float32
BF16 = jnp.bfloat16

HEAD_DIM = 64
N_MIX_HEADS = 12
MIX_WIDTH = N_MIX_HEADS * HEAD_DIM
N_MEM_HEADS = 4
MEM_WIDTH = N_MEM_HEADS * HEAD_DIM
MOBA_BLOCK = 256
MOBA_TOPK = 3
PAGE_SIZE = 128
SWA_GROUPS = ((128, 1), (512, 4), (2048, 16))
SWA_HEADS = 4
SWA_WIDTH = SWA_HEADS * HEAD_DIM
CONV_WIDTH = 31
CONV_HALO = 32
N_MIXERS = 3
ATTN_SCALE = HEAD_DIM ** -0.5
NEG = -1e30
RMS_EPS = 1e-6
LN_EPS = 1e-5

LANES = 128
VMEM_LIMIT = 48 * 1024 * 1024
NT_DIMS = (((1,), (1,)), ((), ()))
HIGHEST = lax.Precision.HIGHEST


def _cparams(n_grid):
    return pltpu.CompilerParams(dimension_semantics=("arbitrary",) * n_grid,
                                vmem_limit_bytes=VMEM_LIMIT)


def _rms(x, g):
    return x * lax.rsqrt(jnp.mean(x * x, axis=-1, keepdims=True) + RMS_EPS) * g


def _dot(a, b, precision=None):
    return jnp.dot(a, b, precision=precision, preferred_element_type=F32)


def _dot_nt(a, b, precision=None):
    return lax.dot_general(a, b, NT_DIMS, precision=precision, preferred_element_type=F32)


def _sigmoid(x):
    return 1.0 / (1.0 + jnp.exp(-x))


def _iota(shape, axis):
    return lax.broadcasted_iota(jnp.int32, shape, axis)


def _in_proj_kernel(x_ref, g_ref, w_ref, *out_refs, splits):
    h = _rms(x_ref[...], g_ref[...]).astype(BF16)
    off = 0
    for o_ref, n in zip(out_refs, splits):
        o_ref[...] = _dot(h, w_ref[:, off:off + n])
        off += n


def _in_proj(x, g, w_bf16, splits, bm):
    m, d = x.shape
    n = w_bf16.shape[1]
    assert sum(splits) == n and m % bm == 0
    return pl.pallas_call(
        functools.partial(_in_proj_kernel, splits=splits),
        grid=(m // bm,),
        in_specs=[pl.BlockSpec((bm, d), lambda i: (i, 0)),
                  pl.BlockSpec((1, d), lambda i: (0, 0)),
                  pl.BlockSpec((d, n), lambda i: (0, 0), pipeline_mode=pl.Buffered(1))],
        out_specs=[pl.BlockSpec((bm, s), lambda i: (i, 0)) for s in splits],
        out_shape=[jax.ShapeDtypeStruct((m, s), F32) for s in splits],
        compiler_params=_cparams(1),
        name="in_proj",
    )(x, g.reshape(1, d), w_bf16)


def _out_ffn_kernel(x_ref, mix_ref, mem_ref, wo_mix_ref, wo_mem_ref, g_ref, wup_ref, wdn_ref,
                    gfin_ref, o_ref, *, ff_chunk, final):
    x = (x_ref[...] + _dot(mix_ref[...].astype(BF16), wo_mix_ref[...])
         + _dot(mem_ref[...].astype(BF16), wo_mem_ref[...]))
    h = _rms(x, g_ref[...]).astype(BF16)
    o_ref[...] = x
    d_ff = wup_ref.shape[1]
    for c in range(d_ff // ff_chunk):
        u = _dot(h, wup_ref[:, c * ff_chunk:(c + 1) * ff_chunk])
        a = jnp.square(jnp.maximum(u, 0.0)).astype(BF16)
        o_ref[...] += _dot(a, wdn_ref[c * ff_chunk:(c + 1) * ff_chunk, :])
    if final:
        o_ref[...] = _rms(o_ref[...], gfin_ref[...])


def _out_ffn(x, mix, mem, wo_bf16, g, wup_bf16, wdn_bf16, g_final, final, bm):
    m, d = x.shape
    wmix = mix.shape[1]
    wmem = mem.shape[1]
    d_ff = wup_bf16.shape[1]
    const = lambda i: (0, 0)
    single = pl.Buffered(1)
    return pl.pallas_call(
        functools.partial(_out_ffn_kernel, ff_chunk=1024, final=final),
        grid=(m // bm,),
        in_specs=[pl.BlockSpec((bm, d), lambda i: (i, 0)),
                  pl.BlockSpec((bm, wmix), lambda i: (i, 0)),
                  pl.BlockSpec((bm, wmem), lambda i: (i, 0)),
                  pl.BlockSpec((wmix, d), const, pipeline_mode=single),
                  pl.BlockSpec((wmem, d), const, pipeline_mode=single),
                  pl.BlockSpec((1, d), const),
                  pl.BlockSpec((d, d_ff), const, pipeline_mode=single),
                  pl.BlockSpec((d_ff, d), const, pipeline_mode=single),
                  pl.BlockSpec((1, d), const)],
        out_specs=pl.BlockSpec((bm, d), lambda i: (i, 0)),
        out_shape=jax.ShapeDtypeStruct((m, d), F32),
        compiler_params=_cparams(1),
        name="out_ffn",
    )(x, mix, mem, wo_bf16[:wmix], wo_bf16[wmix:], g.reshape(1, d), wup_bf16, wdn_bf16,
      g_final.reshape(1, d))


def _mem_kv_kernel(mem_ref, g_ref, w_ref, kt_ref, vt_ref):
    h = _rms(mem_ref[0], g_ref[0]).astype(BF16)
    zt = _dot(h, w_ref[0]).T
    kt_ref[0, 0] = zt[:MEM_WIDTH]
    vt_ref[0, 0] = zt[MEM_WIDTH:]


def _mem_kv(mem, g_mem, w_bf16):
    b, n_mem, d = mem.shape
    depth = w_bf16.shape[0]
    out = jax.ShapeDtypeStruct((depth, b, MEM_WIDTH, n_mem), F32)
    return pl.pallas_call(
        _mem_kv_kernel,
        grid=(depth, b),
        in_specs=[pl.BlockSpec((1, n_mem, d), lambda l, i: (i, 0, 0)),
                  pl.BlockSpec((1, 1, d), lambda l, i: (l, 0, 0)),
                  pl.BlockSpec((1, d, 2 * MEM_WIDTH), lambda l, i: (l, 0, 0))],
        out_specs=[pl.BlockSpec((1, 1, MEM_WIDTH, n_mem), lambda l, i: (l, i, 0, 0))] * 2,
        out_shape=[out, out],
        compiler_params=_cparams(2),
        name="mem_kv",
    )(mem, g_mem.reshape(depth, 1, d), w_bf16)


def _mem_attn_kernel(q_ref, kt_ref, vt_ref, o_ref):
    q = q_ref[0]
    kt = kt_ref[...].astype(BF16)
    vt = vt_ref[...].astype(BF16)
    lane_head = _iota(q.shape, 1) // HEAD_DIM
    out = jnp.zeros(q.shape, F32)
    for h in range(N_MEM_HEADS):
        qh = jnp.where(lane_head == h, q * ATTN_SCALE, 0.0).astype(BF16)
        s = _dot(qh, kt)
        p = jnp.exp(s - jnp.max(s, axis=-1, keepdims=True))
        l = jnp.sum(p, axis=-1, keepdims=True)
        o = _dot_nt(p.astype(BF16), vt)
        out = jnp.where(lane_head == h, o / l, out)
    o_ref[0] = out


def _mem_attn(q3, kt_all, vt_all, layer, tq):
    nb, s, w = q3.shape
    n_mem = kt_all.shape[-1]
    kv_spec = pl.BlockSpec((None, None, w, n_mem), lambda b, i: (layer, b, 0, 0))
    return pl.pallas_call(
        _mem_attn_kernel,
        grid=(nb, s // tq),
        in_specs=[pl.BlockSpec((1, tq, w), lambda b, i: (b, i, 0)), kv_spec, kv_spec],
        out_specs=pl.BlockSpec((1, tq, w), lambda b, i: (b, i, 0)),
        out_shape=jax.ShapeDtypeStruct((nb, s, w), F32),
        compiler_params=_cparams(2),
        name="mem_attn",
    )(q3, kt_all, vt_all)


def _attend(qb, kb, vt, ok):
    s = jnp.where(ok, _dot_nt(kb, qb), NEG)
    m = jnp.max(s, axis=0, keepdims=True)
    p = jnp.exp(s - m)
    l = jnp.sum(p, axis=0, keepdims=True)
    acc = _dot(vt, p.astype(BF16))
    return m, l, acc


def _merge(a, b):
    m = jnp.maximum(a[0], b[0])
    wa = jnp.exp(a[0] - m)
    wb = jnp.exp(b[0] - m)
    return m, wa * a[1] + wb * b[1], wa * a[2] + wb * b[2]


def _moba_past_sizes(nb):
    return sorted({min(n, nb - 1) for n in (2, 4, nb - 1) if nb > 1})


def _moba_prompt_kernel(q_ref, k_ref, v_ref, o_ref, kmean_ref, kb_ref, vt_ref, vtd_ref):
    qi = pl.program_id(2)
    blk = MOBA_BLOCK
    nb = k_ref.shape[1] // blk

    @pl.when(qi == 0)
    def _():
        for n in range(nb):
            rows = slice(n * blk, (n + 1) * blk)
            kblk = k_ref[0, rows, :]
            kmean_ref[n:n + 1, :] = jnp.mean(kblk, axis=0, keepdims=True)
            kb_ref[rows, :] = kblk.astype(BF16)
            vt = v_ref[0, rows, :].T.astype(BF16)
            vt_ref[:, rows] = vt
            vtd_ref[n] = vt

    q = q_ref[0]
    lane = _iota((blk, LANES), 1)
    lane_nb = _iota((nb, LANES), 1)
    blk_id = _iota((nb, blk), 0)
    kmean = kmean_ref[...]
    valid = blk_id < qi
    qbs, sels = [], []
    for e in range(2):
        qe = jnp.where(lane // HEAD_DIM == e, q, 0.0)
        kme = jnp.where(lane_nb // HEAD_DIM == e, kmean, 0.0)
        gate = jnp.where(valid, _dot_nt(kme, qe, precision=HIGHEST), NEG)
        cnt = jnp.zeros((nb, blk), F32)
        for m in range(nb):
            gm = gate[m:m + 1, :]
            beats = (gm > gate) | ((gm == gate) & (blk_id > m))
            cnt = cnt + jnp.where(beats, 1.0, 0.0)
        sels.append(jnp.where((cnt < MOBA_TOPK) & valid, 1.0, 0.0))
        qbs.append((qe * ATTN_SCALE).astype(BF16))

    causal = _iota((blk, blk), 0) <= _iota((blk, blk), 1)
    kd = kb_ref[pl.ds(pl.multiple_of(qi * blk, blk), blk), :]
    vd = vtd_ref[qi]
    feat = _iota((LANES, blk), 0)

    def finish(parts):
        (_, l0, acc0), (_, l1, acc1) = parts
        o_ref[0] = jnp.where(feat < HEAD_DIM, acc0 / l0, acc1 / l1).T

    @pl.when(qi == 0)
    def _():
        finish([_attend(qbs[e], kd, vd, causal) for e in range(2)])

    lo = 0
    for n_past in _moba_past_sizes(nb):
        @pl.when((qi > lo) & (qi <= n_past))
        def _(n_past=n_past):
            parts = []
            for e in range(2):
                ok = jnp.concatenate([jnp.broadcast_to(sels[e][n:n + 1, :], (blk, blk))
                                      for n in range(n_past)], axis=0) > 0.5
                past = _attend(qbs[e], kb_ref[0:n_past * blk, :], vt_ref[:, 0:n_past * blk], ok)
                parts.append(_merge(_attend(qbs[e], kd, vd, causal), past))
            finish(parts)
        lo = n_past


def _moba_prompt(q3, k3, v3):
    b, s, w = q3.shape
    blk = MOBA_BLOCK
    nb = s // blk
    assert s % blk == 0 and w % LANES == 0
    kv_spec = pl.BlockSpec((1, s, LANES), lambda i, p, j: (i, 0, p))
    return pl.pallas_call(
        _moba_prompt_kernel,
        grid=(b, w // LANES, nb),
        in_specs=[pl.BlockSpec((1, blk, LANES), lambda i, p, j: (i, j, p)), kv_spec, kv_spec],
        out_specs=pl.BlockSpec((1, blk, LANES), lambda i, p, j: (i, j, p)),
        out_shape=jax.ShapeDtypeStruct((b, s, w), F32),
        scratch_shapes=[pltpu.VMEM((nb, LANES), F32),
                        pltpu.VMEM((s, LANES), BF16),
                        pltpu.VMEM((LANES, s), BF16),
                        pltpu.VMEM((nb, LANES, blk), BF16)],
        compiler_params=_cparams(3),
        name="moba_prompt",
    )(q3, k3, v3)


def _block_diag_q(q, n_heads, rows_per_tok):
    t_len, w = q.shape
    hrow = _iota((rows_per_tok, w), 0)
    lane_head = _iota((rows_per_tok, w), 1) // HEAD_DIM
    keep = (hrow == lane_head) & (hrow < n_heads)
    parts = [jnp.where(keep, jnp.broadcast_to(q[t:t + 1, :], (rows_per_tok, w)), 0.0)
             for t in range(t_len)]
    return jnp.concatenate(parts, axis=0)


def _head_diag_rows(o, t_len, rows_per_tok):
    w = o.shape[1]
    hrow = _iota((rows_per_tok, w), 0)
    lane_head = _iota((rows_per_tok, w), 1) // HEAD_DIM
    keep = hrow == lane_head
    return [jnp.sum(jnp.where(keep, o[t * rows_per_tok:(t + 1) * rows_per_tok, :], 0.0), axis=0,
                    keepdims=True) for t in range(t_len)]


MOBA_QROWS = 16
MOBA_CHUNK = 8
MOBA_AHEAD = 3
MOBA_SLOTS = MOBA_AHEAD + 1


def _moba_sample_kernel(pt_ref, q_ref, kn_ref, vn_ref, kt_hbm, vt_hbm, o_ref,
                        buf_ref, sem, sc_ref, ksum_ref, acc_ref, *, layer, n_blocks):
    b = pl.program_id(0)
    n_batch = pl.num_programs(0)
    t_len = q_ref.shape[1]
    nr = t_len * MOBA_QROWS
    chunk = MOBA_CHUNK
    k_chunks = 2 * n_blocks // chunk
    per_b = 2 * k_chunks
    total = n_batch * per_b

    def page_copy(src_hbm, bb, page, slot, j):
        return pltpu.make_async_copy(src_hbm.at[layer, pt_ref[bb, page]], buf_ref.at[slot, j],
                                     sem.at[slot])

    def start_chunk(c):
        bb = c // per_b
        i = c % per_b
        slot = c % MOBA_SLOTS

        @pl.when(i < k_chunks)
        def _():
            for j in range(chunk):
                page_copy(kt_hbm, bb, i * chunk + j, slot, j).start()

        @pl.when(i >= k_chunks)
        def _():
            for j in range(chunk):
                page_copy(vt_hbm, bb, (i - k_chunks) * chunk + j, slot, j).start()

    def next_chunk(c):
        @pl.when(c + MOBA_AHEAD < total)
        def _():
            start_chunk(c + MOBA_AHEAD)

        slot = c % MOBA_SLOTS
        for j in range(chunk):
            page_copy(kt_hbm, 0, 0, slot, j).wait()
        return slot

    @pl.when(b == 0)
    def _():
        for c in range(MOBA_AHEAD):
            start_chunk(c)

    qbd = _block_diag_q(q_ref[0], N_MIX_HEADS, MOBA_QROWS)
    qb = (qbd * ATTN_SCALE).astype(BF16)
    lane_w = _iota(ksum_ref.shape, 1)

    def k_body(i, carry):
        slot = next_chunk(b * per_b + i)
        for jj in range(chunk // 2):
            ka = buf_ref[slot, 2 * jj]
            kb = buf_ref[slot, 2 * jj + 1]
            page = i * chunk + 2 * jj
            sc_ref[page] = _dot(qb, ka.astype(BF16))
            sc_ref[page + 1] = _dot(qb, kb.astype(BF16))
            col = jnp.sum(ka + kb, axis=1, keepdims=True)
            ksum_ref[...] = jnp.where(lane_w == page // 2, col, ksum_ref[...])
        return carry

    ksum_ref[...] = jnp.zeros(ksum_ref.shape, F32)
    lax.fori_loop(0, k_chunks, k_body, 0)

    lane = _iota((nr, LANES), 1)
    gmat = _dot(qbd, ksum_ref[...], precision=HIGHEST) * (1.0 / MOBA_BLOCK)
    gmat = jnp.where(lane < n_blocks, gmat, NEG)
    cnt = jnp.zeros((nr, LANES), F32)
    for m in range(n_blocks):
        gm = jnp.broadcast_to(gmat[:, m:m + 1], (nr, LANES))
        beats = (gm > gmat) | ((gm == gmat) & (lane > m))
        cnt = cnt + jnp.where(beats, 1.0, 0.0)
    sel = jnp.where((cnt < MOBA_TOPK) & (lane < n_blocks), 1.0, 0.0)

    def block_ok(n):
        return jnp.broadcast_to(sel[:, n:n + 1], (nr, LANES)) > 0.5

    trow = _iota((nr, 1), 0) // MOBA_QROWS
    s_own = []
    for u in range(t_len):
        su = jnp.sum(qbd * kn_ref[0, u:u + 1, :], axis=1, keepdims=True) * ATTN_SCALE
        s_own.append(jnp.where(trow >= u, su, NEG))

    mx = jnp.full((nr, LANES), NEG, F32)
    for n in range(n_blocks):
        ok = block_ok(n)
        for half in range(2):
            mx = jnp.maximum(mx, jnp.where(ok, sc_ref[2 * n + half], NEG))
    m = jnp.max(mx, axis=1, keepdims=True)
    for su in s_own:
        m = jnp.maximum(m, su)
    lsum = jnp.zeros((nr, LANES), F32)
    for n in range(n_blocks):
        ok = block_ok(n)
        for half in range(2):
            p = jnp.exp(jnp.where(ok, sc_ref[2 * n + half], NEG) - m)
            sc_ref[2 * n + half] = p
            lsum = lsum + p
    l = jnp.sum(lsum, axis=1, keepdims=True)
    acc = jnp.zeros(acc_ref.shape, F32)
    for u in range(t_len):
        pu = jnp.exp(s_own[u] - m)
        l = l + pu
        acc = acc + pu * vn_ref[0, u:u + 1, :]
    acc_ref[...] = acc

    def v_body(i, carry):
        slot = next_chunk(b * per_b + k_chunks + i)
        for j in range(chunk):
            p = sc_ref[i * chunk + j].astype(BF16)
            acc_ref[...] += _dot_nt(p, buf_ref[slot, j].astype(BF16))
        return carry

    lax.fori_loop(0, k_chunks, v_body, 0)

    o = acc_ref[...] * (1.0 / l)
    for t, r in enumerate(_head_diag_rows(o, t_len, MOBA_QROWS)):
        o_ref[0, t:t + 1, :] = r


def _moba_sample(q3, kn3, vn3, ckt, cvt, layer, page_table):
    bd, t_len, w = q3.shape
    n_pages = page_table.shape[1]
    pages_per_block = MOBA_BLOCK // PAGE_SIZE
    assert pages_per_block == 2 and n_pages % pages_per_block == 0 and t_len <= MOBA_BLOCK
    nb = n_pages // pages_per_block
    assert n_pages % MOBA_CHUNK == 0 and MOBA_CHUNK % pages_per_block == 0 and nb <= LANES
    nr = t_len * MOBA_QROWS
    tok = pl.BlockSpec((1, t_len, w), lambda b, pt: (b, 0, 0))
    hbm = pl.BlockSpec(memory_space=pl.ANY)
    grid_spec = pltpu.PrefetchScalarGridSpec(
        num_scalar_prefetch=1,
        grid=(bd,),
        in_specs=[tok, tok, tok, hbm, hbm],
        out_specs=tok,
        scratch_shapes=[pltpu.VMEM((MOBA_SLOTS, MOBA_CHUNK, w, PAGE_SIZE), F32),
                        pltpu.SemaphoreType.DMA((MOBA_SLOTS,)),
                        pltpu.VMEM((n_pages, nr, PAGE_SIZE), F32),
                        pltpu.VMEM((w, LANES), F32),
                        pltpu.VMEM((nr, w), F32)])
    return pl.pallas_call(
        functools.partial(_moba_sample_kernel, layer=layer, n_blocks=nb),
        grid_spec=grid_spec,
        out_shape=jax.ShapeDtypeStruct((bd, t_len, w), F32),
        compiler_params=_cparams(1),
        name="moba_sample",
    )(page_table, q3, kn3, vn3, ckt, cvt)


def _dil_prompt_kernel(q0_ref, q1_ref, q2_ref, k0_ref, k1_ref, k2_ref, v0_ref, v1_ref, v2_ref,
                       o_ref, kb_ref, vt_ref):
    qi = pl.program_id(2)
    tq = q0_ref.shape[1]
    s_len = k0_ref.shape[1]
    q_refs = (q0_ref, q1_ref, q2_ref)

    @pl.when(qi == 0)
    def _():
        for g, (k_ref, v_ref) in enumerate(((k0_ref, v0_ref), (k1_ref, v1_ref), (k2_ref, v2_ref))):
            kb_ref[g] = k_ref[0].astype(BF16)
            for n in range(s_len // LANES):
                vt_ref[g, n] = v_ref[0, n * LANES:(n + 1) * LANES, :].T.astype(BF16)

    lane = _iota((tq, LANES), 1)
    t0 = qi * tq
    os = [[], []]
    lses = [[], []]
    for g, (window, dil) in enumerate(SWA_GROUPS):
        n_keys = min(window + tq, s_len)
        k0 = pl.multiple_of(jnp.maximum(t0 + tq - n_keys, 0), LANES)
        d = (t0 - k0) + _iota((n_keys, tq), 1) - _iota((n_keys, tq), 0)
        ok = (d >= 0) & (d <= window)
        if dil > 1:
            ok = ok & ((d & (dil - 1)) == 0)
        kb = kb_ref[g, pl.ds(k0, n_keys), :]
        vt = jnp.concatenate([vt_ref[g, k0 // LANES + i] for i in range(n_keys // LANES)], axis=1)
        for e in range(2):
            qb = jnp.where(lane // HEAD_DIM == e, q_refs[g][0] * ATTN_SCALE, 0.0).astype(BF16)
            m, l, acc = _attend(qb, kb, vt, ok)
            os[e].append(acc / l)
            lses[e].append(m + jnp.log(l))
    outs = []
    for e in range(2):
        mx = jnp.maximum(jnp.maximum(lses[e][0], lses[e][1]), lses[e][2])
        ws = [jnp.exp(x - mx) for x in lses[e]]
        tot = ws[0] + ws[1] + ws[2]
        outs.append((ws[0] * os[e][0] + ws[1] * os[e][1] + ws[2] * os[e][2]) / tot)
    feat = _iota((LANES, tq), 0)
    o_ref[0] = jnp.where(feat < HEAD_DIM, outs[0], outs[1]).T


def _dil_prompt(q3, k3, v3, tq):
    b, s, w = q3.shape
    n_pairs = SWA_WIDTH // LANES
    assert w == len(SWA_GROUPS) * SWA_WIDTH and s % tq == 0
    assert all(d & (d - 1) == 0 and wd % LANES == 0 for wd, d in SWA_GROUPS) and tq % LANES == 0

    def q_spec(g):
        return pl.BlockSpec((1, tq, LANES), lambda i, p, j: (i, j, g * n_pairs + p))

    def kv_spec(g):
        return pl.BlockSpec((1, s, LANES), lambda i, p, j: (i, 0, g * n_pairs + p))

    n_g = len(SWA_GROUPS)
    return pl.pallas_call(
        _dil_prompt_kernel,
        grid=(b, n_pairs, s // tq),
        in_specs=([q_spec(g) for g in range(n_g)] + [kv_spec(g) for g in range(n_g)]
                  + [kv_spec(g) for g in range(n_g)]),
        out_specs=pl.BlockSpec((1, tq, LANES), lambda i, p, j: (i, j, p)),
        out_shape=jax.ShapeDtypeStruct((b, s, SWA_WIDTH), F32),
        scratch_shapes=[pltpu.VMEM((n_g, s, LANES), BF16),
                        pltpu.VMEM((n_g, s // LANES, LANES, LANES), BF16)],
        compiler_params=_cparams(3),
        name="dil_prompt",
    )(q3, q3, q3, k3, k3, k3, v3, v3, v3)


SWA_QROWS = 8


def _dil_sample_kernel(q_ref, kn_ref, vn_ref, b0_ref, b1_ref, b2_ref, o_ref):
    t_len = q_ref.shape[1]
    nr = t_len * SWA_QROWS
    q = q_ref[0]
    kn = kn_ref[0]
    vn = vn_ref[0]
    trow = _iota((nr, 1), 0) // SWA_QROWS
    os, lses = [], []
    for g, ((window, dil), buf_ref) in enumerate(zip(SWA_GROUPS, (b0_ref, b1_ref, b2_ref))):
        sl = slice(g * SWA_WIDTH, (g + 1) * SWA_WIDTH)
        qbd = _block_diag_q(q[:, sl], SWA_HEADS, SWA_QROWS)
        kt = buf_ref[0, 0].astype(BF16)
        vt = buf_ref[0, 1].astype(BF16)
        s = _dot((qbd * ATTN_SCALE).astype(BF16), kt)
        d = window + (_iota((nr, window), 0) // SWA_QROWS) - _iota((nr, window), 1)
        s = jnp.where((d <= window) & ((d & (dil - 1)) == 0), s, NEG)
        s_new = []
        for u in range(t_len):
            su = jnp.sum(qbd * kn[u:u + 1, sl], axis=1, keepdims=True) * ATTN_SCALE
            du = trow - u
            s_new.append(jnp.where((du >= 0) & ((du & (dil - 1)) == 0), su, NEG))
        m = jnp.max(s, axis=1, keepdims=True)
        for su in s_new:
            m = jnp.maximum(m, su)
        p = jnp.exp(s - m)
        l = jnp.sum(p, axis=1, keepdims=True)
        o = _dot_nt(p.astype(BF16), vt)
        for u in range(t_len):
            pu = jnp.exp(s_new[u] - m)
            l = l + pu
            o = o + pu * vn[u:u + 1, sl]
        os.append(o / l)
        lses.append(m + jnp.log(l))
    mx = jnp.maximum(jnp.maximum(lses[0], lses[1]), lses[2])
    ws = [jnp.exp(x - mx) for x in lses]
    tot = ws[0] + ws[1] + ws[2]
    out = (ws[0] * os[0] + ws[1] * os[1] + ws[2] * os[2]) / tot
    for t, r in enumerate(_head_diag_rows(out, t_len, SWA_QROWS)):
        o_ref[0, t:t + 1, :] = r


def _dil_sample(q3, kn3, vn3, bufs_t, layer):
    bd, t_len, w = q3.shape
    tok = pl.BlockSpec((1, t_len, w), lambda b: (b, 0, 0))
    buf_specs = []
    for (window, _), buf in zip(SWA_GROUPS, bufs_t):
        assert buf.shape[1:] == (bd, 2, SWA_WIDTH, window)
        buf_specs.append(pl.BlockSpec((None, 1, 2, SWA_WIDTH, window),
                                      lambda b: (layer, b, 0, 0, 0)))
    return pl.pallas_call(
        _dil_sample_kernel,
        grid=(bd,),
        in_specs=[tok, tok, tok] + buf_specs,
        out_specs=pl.BlockSpec((1, t_len, SWA_WIDTH), lambda b: (b, 0, 0)),
        out_shape=jax.ShapeDtypeStruct((bd, t_len, SWA_WIDTH), F32),
        compiler_params=_cparams(1),
        name="dil_sample",
    )(q3, kn3, vn3, *bufs_t)


def _ln_swish(y, lg, lb):
    mu = jnp.mean(y, axis=-1, keepdims=True)
    yc = y - mu
    var = jnp.mean(yc * yc, axis=-1, keepdims=True)
    z = yc * lax.rsqrt(var + LN_EPS) * lg + lb
    return z * _sigmoid(z)


def _conv_prompt_kernel(a_ref, g_ref, w_ref, b_ref, lg_ref, lb_ref, o_ref, tail_ref, ctx_ref):
    qi = pl.program_id(1)
    tq = a_ref.shape[1]
    halo = CONV_HALO

    @pl.when(qi == 0)
    def _():
        ctx_ref[0:halo, :] = jnp.zeros((halo, ctx_ref.shape[1]), F32)

    ctx_ref[halo:halo + tq, :] = a_ref[0] * _sigmoid(g_ref[0])
    first = halo - (CONV_WIDTH - 1)
    acc = jnp.zeros((tq, ctx_ref.shape[1]), F32)
    for j in range(CONV_WIDTH):
        acc = acc + ctx_ref[first + j:first + j + tq, :] * w_ref[j:j + 1, :]
    o_ref[0] = _ln_swish(acc + b_ref[...], lg_ref[...], lb_ref[...])
    tail = ctx_ref[tq:tq + halo, :]
    ctx_ref[0:halo, :] = tail
    tail_ref[0] = tail


def _conv_prompt(a3, g3, w, b, lg, lb, tq):
    bsz, s, c = a3.shape
    assert s % tq == 0 and tq >= CONV_HALO
    tile = pl.BlockSpec((1, tq, c), lambda i, j: (i, j, 0))
    vec = pl.BlockSpec((1, c), lambda i, j: (0, 0))
    return pl.pallas_call(
        _conv_prompt_kernel,
        grid=(bsz, s // tq),
        in_specs=[tile, tile, pl.BlockSpec((CONV_WIDTH, c), lambda i, j: (0, 0)), vec, vec, vec],
        out_specs=[tile, pl.BlockSpec((1, CONV_HALO, c), lambda i, j: (i, 0, 0))],
        out_shape=[jax.ShapeDtypeStruct((bsz, s, c), F32),
                   jax.ShapeDtypeStruct((bsz, CONV_HALO, c), F32)],
        scratch_shapes=[pltpu.VMEM((CONV_HALO + tq, c), F32)],
        compiler_params=_cparams(2),
        name="conv_prompt",
    )(a3, g3, w, b.reshape(1, c), lg.reshape(1, c), lb.reshape(1, c))


def _conv_sample_kernel(st_ref, a_ref, g_ref, w_ref, b_ref, lg_ref, lb_ref, o_ref, ns_ref):
    n_hist = st_ref.shape[0]
    t_len = a_ref.shape[0]
    u = [a_ref[t] * _sigmoid(g_ref[t]) for t in range(t_len)]

    def ctx(i):
        return st_ref[i] if i < n_hist else u[i - n_hist]

    for t in range(t_len):
        acc = jnp.zeros(u[0].shape, F32)
        for j in range(CONV_WIDTH):
            acc = acc + ctx(t + j) * w_ref[j:j + 1, :]
        o_ref[t] = _ln_swish(acc + b_ref[...], lg_ref[...], lb_ref[...])
    for i in range(n_hist):
        ns_ref[i] = ctx(i + t_len)


def _conv_sample(state_t, a_t, g_t, w, b, lg, lb, layer):
    _, n_hist, bd, c = state_t.shape
    t_len = a_t.shape[0]
    assert n_hist == CONV_WIDTH - 1
    tok = pl.BlockSpec((t_len, bd, c), lambda i: (0, 0, 0))
    vec = pl.BlockSpec((1, c), lambda i: (0, 0))
    return pl.pallas_call(
        _conv_sample_kernel,
        grid=(1,),
        in_specs=[pl.BlockSpec((None, n_hist, bd, c), lambda i: (layer, 0, 0, 0)), tok, tok,
                  pl.BlockSpec((CONV_WIDTH, c), lambda i: (0, 0)), vec, vec, vec],
        out_specs=[tok, pl.BlockSpec((n_hist, bd, c), lambda i: (0, 0, 0))],
        out_shape=[jax.ShapeDtypeStruct((t_len, bd, c), F32),
                   jax.ShapeDtypeStruct((n_hist, bd, c), F32)],
        compiler_params=_cparams(1),
        name="conv_sample",
    )(state_t, a_t, g_t, w, b.reshape(1, c), lg.reshape(1, c), lb.reshape(1, c))


def kernel(x_prompt, x_sample, mem_prompt, cache_moba_k, cache_moba_v, page_table, state_swa_w128, state_swa_w512, state_swa_w2048, state_conv, cache_mem_k, cache_mem_v, g_mix, g_mem, w_mem_kv, w_in_a, w_out_a, w_in_b, w_out_b, w_in_c, conv_w, conv_b, conv_ln_g, conv_ln_b, w_out_c, g_ffn, w_ffn_up, w_ffn_down, g_final):
    b, s, d = x_prompt.shape
    bd, t_len, _ = x_sample.shape
    depth = g_mix.shape[0]
    kinds = tuple(i % N_MIXERS for i in range(depth))
    past_len = page_table.shape[1] * PAGE_SIZE
    n_mem = mem_prompt.shape[1]
    bm = 512
    bf = lambda w: w.astype(BF16)

    xp = x_prompt.reshape(b * s, d)
    xs = x_sample.reshape(bd * t_len, d)

    mem_kt, mem_vt = _mem_kv(mem_prompt, g_mem, bf(w_mem_kv))
    cmem_kt = cache_mem_k.transpose(0, 1, 3, 4, 2).reshape(depth, bd, MEM_WIDTH, n_mem)
    cmem_vt = cache_mem_v.transpose(0, 1, 3, 4, 2).reshape(depth, bd, MEM_WIDTH, n_mem)
    n_pool = cache_moba_k.shape[1]
    ckt = cache_moba_k.transpose(0, 1, 3, 4, 2).reshape(-1, n_pool, MIX_WIDTH, PAGE_SIZE)
    cvt = cache_moba_v.transpose(0, 1, 3, 4, 2).reshape(-1, n_pool, MIX_WIDTH, PAGE_SIZE)
    swa_states = (state_swa_w128, state_swa_w512, state_swa_w2048)
    swa_t = [st.transpose(0, 1, 3, 4, 5, 2).reshape(st.shape[0], bd, 2, SWA_WIDTH, st.shape[2])
             for st in swa_states]
    conv_t = state_conv.transpose(0, 2, 1, 3)

    moba_kp, moba_vp, moba_ks, moba_vs = [], [], [], []
    swa_p = [[] for _ in SWA_GROUPS]
    swa_s = [[] for _ in SWA_GROUPS]
    conv_p, conv_s = [], []

    for l in range(depth):
        kind = kinds[l]
        j = kinds[:l].count(kind)
        if kind == 2:
            w_in = bf(w_in_c[j])
            splits = (MIX_WIDTH, MIX_WIDTH, MEM_WIDTH)
            ap, gp, mqp = _in_proj(xp, g_mix[l], w_in, splits, bm)
            a_s, g_s, mqs = _in_proj(xs, g_mix[l], w_in, splits, bd * t_len)
            mix_p, tail = _conv_prompt(ap.reshape(b, s, -1), gp.reshape(b, s, -1), conv_w[j],
                                       conv_b[j], conv_ln_g[j], conv_ln_b[j], 256)
            mix_p = mix_p.reshape(b * s, -1)
            tb = lambda z: z.reshape(bd, t_len, -1).transpose(1, 0, 2)
            mix_s, new_state = _conv_sample(conv_t, tb(a_s), tb(g_s), conv_w[j], conv_b[j],
                                            conv_ln_g[j], conv_ln_b[j], j)
            mix_s = mix_s.transpose(1, 0, 2).reshape(bd * t_len, -1)
            conv_p.append(tail[:, CONV_HALO - (CONV_WIDTH - 1):])
            conv_s.append(new_state.transpose(1, 0, 2))
            w_out = bf(w_out_c[j])
        else:
            w_in = bf(w_in_a[j] if kind == 0 else w_in_b[j])
            splits = (MIX_WIDTH, MIX_WIDTH, MIX_WIDTH, MEM_WIDTH)
            qp, kp, vp, mqp = _in_proj(xp, g_mix[l], w_in, splits, bm)
            qs, ks, vs, mqs = _in_proj(xs, g_mix[l], w_in, splits, bd * t_len)
            qp3, kp3, vp3 = (z.reshape(b, s, MIX_WIDTH) for z in (qp, kp, vp))
            qs3, ks3, vs3 = (z.reshape(bd, t_len, MIX_WIDTH) for z in (qs, ks, vs))
            if kind == 0:
                mix_p = _moba_prompt(qp3, kp3, vp3)
                mix_s = _moba_sample(qs3, ks3, vs3, ckt, cvt, j, page_table)
                moba_kp.append(kp.reshape(b, s, N_MIX_HEADS, HEAD_DIM))
                moba_vp.append(vp.reshape(b, s, N_MIX_HEADS, HEAD_DIM))
                moba_ks.append(ks.reshape(bd, t_len, N_MIX_HEADS, HEAD_DIM))
                moba_vs.append(vs.reshape(bd, t_len, N_MIX_HEADS, HEAD_DIM))
                w_out = bf(w_out_a[j])
            else:
                mix_p = _dil_prompt(qp3, kp3, vp3, 256)
                mix_s = _dil_sample(qs3, ks3, vs3, swa_t, j)
                for g, (window, _) in enumerate(SWA_GROUPS):
                    sl = slice(g * SWA_WIDTH, (g + 1) * SWA_WIDTH)
                    heads = lambda z: z.reshape(z.shape[0], z.shape[1], SWA_HEADS, HEAD_DIM)
                    kv_p = jnp.stack([heads(kp3[:, :, sl]), heads(vp3[:, :, sl])], axis=2)
                    swa_p[g].append(kv_p[:, s - min(window, s):])
                    kv_s = jnp.stack([heads(ks3[:, :, sl]), heads(vs3[:, :, sl])], axis=2)
                    assert swa_states[g].shape[2] == window
                    full = jnp.concatenate([swa_states[g][j], kv_s], axis=1)
                    swa_s[g].append(full[:, full.shape[1] - min(window, past_len + t_len):])
                w_out = bf(w_out_b[j])
            mix_p = mix_p.reshape(b * s, -1)
            mix_s = mix_s.reshape(bd * t_len, -1)
        mem_p = _mem_attn(mqp.reshape(b, s, MEM_WIDTH), mem_kt, mem_vt, l, 512)
        mem_s = _mem_attn(mqs.reshape(bd, t_len, MEM_WIDTH), cmem_kt, cmem_vt, l, t_len)
        final = l == depth - 1
        wup, wdn = bf(w_ffn_up[l]), bf(w_ffn_down[l])
        xp = _out_ffn(xp, mix_p, mem_p.reshape(b * s, MEM_WIDTH), w_out, g_ffn[l], wup, wdn,
                      g_final, final, bm)
        xs = _out_ffn(xs, mix_s, mem_s.reshape(bd * t_len, MEM_WIDTH), w_out, g_ffn[l], wup, wdn,
                      g_final, final, bd * t_len)

    heads_t = lambda zt: zt.reshape(depth, b, N_MEM_HEADS, HEAD_DIM, n_mem).transpose(0, 1, 4, 2, 3)
    return (xp.reshape(b, s, d), xs.reshape(bd, t_len, d),
            jnp.stack(moba_kp), jnp.stack(moba_vp), jnp.stack(moba_ks), jnp.stack(moba_vs),
            jnp.stack(swa_p[0]), jnp.stack(swa_p[1]), jnp.stack(swa_p[2]),
            jnp.stack(swa_s[0]), jnp.stack(swa_s[1]), jnp.stack(swa_s[2]),
            jnp.stack(conv_p), jnp.stack(conv_s),
            heads_t(mem_kt), heads_t(mem_vt))
```

```python
import functools

import jax
import jax.numpy as jnp
from jax import lax
from jax.experimental import pallas as pl
from jax.experimental.pallas import tpu as pltpu

F32 = jnp.float32
BF16 = jnp.bfloat16

HEAD_DIM = 64
N_MIX_HEADS = 12
MIX_WIDTH = N_MIX_HEADS * HEAD_DIM
N_MEM_HEADS = 4
MEM_WIDTH = N_MEM_HEADS * HEAD_DIM
MOBA_BLOCK = 256
MOBA_TOPK = 3
PAGE_SIZE = 128
SWA_GROUPS = ((128, 1), (512, 4), (2048, 16))
SWA_HEADS = 4
SWA_WIDTH = SWA_HEADS * HEAD_DIM
CONV_WIDTH = 31
CONV_HALO = 32
SUBLANES = 8
N_MIXERS = 3
ATTN_SCALE = HEAD_DIM ** -0.5
NEG = -1e30
RMS_EPS = 1e-6
LN_EPS = 1e-5

LANES = 128
VMEM_LIMIT = 48 * 1024 * 1024
NT_DIMS = (((1,), (1,)), ((), ()))
HIGHEST = lax.Precision.HIGHEST


def _cparams(n_grid):
    return pltpu.CompilerParams(dimension_semantics=("arbitrary",) * n_grid,
                                vmem_limit_bytes=VMEM_LIMIT)


def _rms(x, g):
    return x * lax.rsqrt(jnp.mean(x * x, axis=-1, keepdims=True) + RMS_EPS) * g


def _dot(a, b, precision=None):
    return jnp.dot(a, b, precision=precision, preferred_element_type=F32)


def _dot_nt(a, b, precision=None):
    return lax.dot_general(a, b, NT_DIMS, precision=precision, preferred_element_type=F32)


def _sigmoid(x):
    return 1.0 / (1.0 + jnp.exp(-x))


def _iota(shape, axis):
    return lax.broadcasted_iota(jnp.int32, shape, axis)


def _in_proj_kernel(x_ref, g_ref, w_ref, *out_refs, splits):
    h = _rms(x_ref[...], g_ref[...]).astype(BF16)
    off = 0
    for o_ref, n in zip(out_refs, splits):
        o_ref[...] = _dot(h, w_ref[:, off:off + n])
        off += n


def _in_proj(x, g, w_bf16, splits, bm):
    m, d = x.shape
    n = w_bf16.shape[1]
    assert sum(splits) == n and m % bm == 0
    return pl.pallas_call(
        functools.partial(_in_proj_kernel, splits=splits),
        grid=(m // bm,),
        in_specs=[pl.BlockSpec((bm, d), lambda i: (i, 0)),
                  pl.BlockSpec((1, d), lambda i: (0, 0)),
                  pl.BlockSpec((d, n), lambda i: (0, 0), pipeline_mode=pl.Buffered(1))],
        out_specs=[pl.BlockSpec((bm, s), lambda i: (i, 0)) for s in splits],
        out_shape=[jax.ShapeDtypeStruct((m, s), F32) for s in splits],
        compiler_params=_cparams(1),
        name="in_proj",
    )(x, g.reshape(1, d), w_bf16)


def _out_ffn_kernel(x_ref, mix_ref, mem_ref, wo_mix_ref, wo_mem_ref, g_ref, wup_ref, wdn_ref,
                    gfin_ref, o_ref, *, ff_chunk, final):
    x = (x_ref[...] + _dot(mix_ref[...].astype(BF16), wo_mix_ref[...])
         + _dot(mem_ref[...].astype(BF16), wo_mem_ref[...]))
    h = _rms(x, g_ref[...]).astype(BF16)
    o_ref[...] = x
    d_ff = wup_ref.shape[1]
    for c in range(d_ff // ff_chunk):
        u = _dot(h, wup_ref[:, c * ff_chunk:(c + 1) * ff_chunk])
        a = jnp.square(jnp.maximum(u, 0.0)).astype(BF16)
        o_ref[...] += _dot(a, wdn_ref[c * ff_chunk:(c + 1) * ff_chunk, :])
    if final:
        o_ref[...] = _rms(o_ref[...], gfin_ref[...])


def _out_ffn(x, mix, mem, wo_bf16, g, wup_bf16, wdn_bf16, g_final, final, bm):
    m, d = x.shape
    wmix = mix.shape[1]
    wmem = mem.shape[1]
    d_ff = wup_bf16.shape[1]
    const = lambda i: (0, 0)
    single = pl.Buffered(1)
    return pl.pallas_call(
        functools.partial(_out_ffn_kernel, ff_chunk=1024, final=final),
        grid=(m // bm,),
        in_specs=[pl.BlockSpec((bm, d), lambda i: (i, 0)),
                  pl.BlockSpec((bm, wmix), lambda i: (i, 0)),
                  pl.BlockSpec((bm, wmem), lambda i: (i, 0)),
                  pl.BlockSpec((wmix, d), const, pipeline_mode=single),
                  pl.BlockSpec((wmem, d), const, pipeline_mode=single),
                  pl.BlockSpec((1, d), const),
                  pl.BlockSpec((d, d_ff), const, pipeline_mode=single),
                  pl.BlockSpec((d_ff, d), const, pipeline_mode=single),
                  pl.BlockSpec((1, d), const)],
        out_specs=pl.BlockSpec((bm, d), lambda i: (i, 0)),
        out_shape=jax.ShapeDtypeStruct((m, d), F32),
        compiler_params=_cparams(1),
        name="out_ffn",
    )(x, mix, mem, wo_bf16[:wmix], wo_bf16[wmix:], g.reshape(1, d), wup_bf16, wdn_bf16,
      g_final.reshape(1, d))


def _mem_kv_kernel(mem_ref, g_ref, w_ref, kt_ref, vt_ref):
    h = _rms(mem_ref[0], g_ref[0]).astype(BF16)
    zt = _dot(h, w_ref[0]).T
    kt_ref[0, 0] = zt[:MEM_WIDTH]
    vt_ref[0, 0] = zt[MEM_WIDTH:]


def _mem_kv(mem, g_mem, w_bf16):
    b, n_mem, d = mem.shape
    depth = w_bf16.shape[0]
    out = jax.ShapeDtypeStruct((depth, b, MEM_WIDTH, n_mem), F32)
    return pl.pallas_call(
        _mem_kv_kernel,
        grid=(depth, b),
        in_specs=[pl.BlockSpec((1, n_mem, d), lambda l, i: (i, 0, 0)),
                  pl.BlockSpec((1, 1, d), lambda l, i: (l, 0, 0)),
                  pl.BlockSpec((1, d, 2 * MEM_WIDTH), lambda l, i: (l, 0, 0))],
        out_specs=[pl.BlockSpec((1, 1, MEM_WIDTH, n_mem), lambda l, i: (l, i, 0, 0))] * 2,
        out_shape=[out, out],
        compiler_params=_cparams(2),
        name="mem_kv",
    )(mem, g_mem.reshape(depth, 1, d), w_bf16)


def _mem_attn_kernel(q_ref, kt_ref, vt_ref, o_ref):
    q = q_ref[0]
    kt = kt_ref[...].astype(BF16)
    vt = vt_ref[...].astype(BF16)
    lane_head = _iota(q.shape, 1) // HEAD_DIM
    out = jnp.zeros(q.shape, F32)
    for h in range(N_MEM_HEADS):
        qh = jnp.where(lane_head == h, q * ATTN_SCALE, 0.0).astype(BF16)
        s = _dot(qh, kt)
        p = jnp.exp(s - jnp.max(s, axis=-1, keepdims=True))
        l = jnp.sum(p, axis=-1, keepdims=True)
        o = _dot_nt(p.astype(BF16), vt)
        out = jnp.where(lane_head == h, o / l, out)
    o_ref[0] = out


def _mem_attn(q3, kt_all, vt_all, layer, tq):
    nb, s, w = q3.shape
    n_mem = kt_all.shape[-1]
    kv_spec = pl.BlockSpec((None, None, w, n_mem), lambda b, i: (layer, b, 0, 0))
    return pl.pallas_call(
        _mem_attn_kernel,
        grid=(nb, s // tq),
        in_specs=[pl.BlockSpec((1, tq, w), lambda b, i: (b, i, 0)), kv_spec, kv_spec],
        out_specs=pl.BlockSpec((1, tq, w), lambda b, i: (b, i, 0)),
        out_shape=jax.ShapeDtypeStruct((nb, s, w), F32),
        compiler_params=_cparams(2),
        name="mem_attn",
    )(q3, kt_all, vt_all)


LOG2E = 1.4426950408889634
Q_SCALE_LOG2 = ATTN_SCALE * LOG2E


def _attend(qb, kb, vt, ok):
    s = jnp.where(ok, _dot_nt(kb, qb), NEG)
    m = jnp.max(s, axis=0, keepdims=True)
    p = jnp.exp2(s - m)
    l = jnp.sum(p, axis=0, keepdims=True)
    acc = _dot(vt, p.astype(BF16))
    return m, l, acc


def _moba_prompt_kernel(q_ref, k_ref, v_ref, o_ref, kmean_ref, kb_ref, vt_ref):
    qi = pl.program_id(2)
    blk = MOBA_BLOCK
    nb = k_ref.shape[1] // blk

    @pl.when(qi == 0)
    def _():
        for n in range(nb):
            rows = slice(n * blk, (n + 1) * blk)
            kblk = k_ref[0, rows, :]
            kmean_ref[n:n + 1, :] = jnp.mean(kblk, axis=0, keepdims=True)
            kb_ref[rows, :] = kblk.astype(BF16)
            vt_ref[:, rows] = v_ref[0, rows, :].T.astype(BF16)

    q = q_ref[0]
    lane = _iota((blk, LANES), 1)
    lane_nb = _iota((nb, LANES), 1)
    blk_id = _iota((nb, blk), 0)
    kmean = kmean_ref[...]
    valid = blk_id < qi
    qbs, sels = [], []
    for e in range(2):
        qe = jnp.where(lane // HEAD_DIM == e, q, 0.0)
        kme = jnp.where(lane_nb // HEAD_DIM == e, kmean, 0.0)
        gate = jnp.where(valid, _dot_nt(kme, qe, precision=HIGHEST), NEG)
        cnt = jnp.zeros((nb, blk), F32)
        for m in range(nb):
            gm = gate[m:m + 1, :]
            beats = (gm > gate) | ((gm == gate) & (blk_id > m))
            cnt = cnt + jnp.where(beats, 1.0, 0.0)
        sels.append(jnp.where((cnt < MOBA_TOPK) & valid, 1.0, 0.0))
        qbs.append((qe * Q_SCALE_LOG2).astype(BF16))

    qb2 = jnp.concatenate(qbs, axis=0)
    sel2 = jnp.concatenate(sels, axis=1)
    causal = _iota((blk, blk), 0) <= _iota((blk, blk), 1)
    causal2 = jnp.where(jnp.concatenate([causal, causal], axis=1), 1.0, 0.0)
    feat = _iota((LANES, blk), 0)

    for n_past in range(nb):
        @pl.when(qi == n_past)
        def _(n_past=n_past):
            n_keys = (n_past + 1) * blk
            ok = jnp.concatenate([jnp.broadcast_to(sel2[n:n + 1, :], (blk, 2 * blk))
                                  for n in range(n_past)] + [causal2], axis=0) > 0.5
            _, l, acc = _attend(qb2, kb_ref[0:n_keys, :], vt_ref[:, 0:n_keys], ok)
            out = acc / l
            o_ref[0] = jnp.where(feat < HEAD_DIM, out[:, :blk], out[:, blk:]).T


def _moba_prompt(q3, k3, v3):
    b, s, w = q3.shape
    blk = MOBA_BLOCK
    nb = s // blk
    assert s % blk == 0 and w % LANES == 0
    kv_spec = pl.BlockSpec((1, s, LANES), lambda i, p, j: (i, 0, p))
    return pl.pallas_call(
        _moba_prompt_kernel,
        grid=(b, w // LANES, nb),
        in_specs=[pl.BlockSpec((1, blk, LANES), lambda i, p, j: (i, j, p)), kv_spec, kv_spec],
        out_specs=pl.BlockSpec((1, blk, LANES), lambda i, p, j: (i, j, p)),
        out_shape=jax.ShapeDtypeStruct((b, s, w), F32),
        scratch_shapes=[pltpu.VMEM((nb, LANES), F32),
                        pltpu.VMEM((s, LANES), BF16),
                        pltpu.VMEM((LANES, s), BF16)],
        compiler_params=_cparams(3),
        name="moba_prompt",
    )(q3, k3, v3)


def _block_diag_q(q, n_heads, rows_per_tok):
    t_len, w = q.shape
    hrow = _iota((rows_per_tok, w), 0)
    lane_head = _iota((rows_per_tok, w), 1) // HEAD_DIM
    keep = (hrow == lane_head) & (hrow < n_heads)
    parts = [jnp.where(keep, jnp.broadcast_to(q[t:t + 1, :], (rows_per_tok, w)), 0.0)
             for t in range(t_len)]
    return jnp.concatenate(parts, axis=0)


def _head_diag_rows(o, t_len, rows_per_tok):
    w = o.shape[1]
    hrow = _iota((rows_per_tok, w), 0)
    lane_head = _iota((rows_per_tok, w), 1) // HEAD_DIM
    keep = hrow == lane_head
    return [jnp.sum(jnp.where(keep, o[t * rows_per_tok:(t + 1) * rows_per_tok, :], 0.0), axis=0,
                    keepdims=True) for t in range(t_len)]


MOBA_QROWS = 16
MOBA_CHUNK = 8
MOBA_AHEAD = 3
MOBA_SLOTS = MOBA_AHEAD + 1


def _moba_sample_kernel(pt_ref, q_ref, kn_ref, vn_ref, kt_hbm, vt_hbm, o_ref,
                        buf_ref, sem, sc_ref, ksum_ref, acc_ref, *, layer, n_blocks):
    b = pl.program_id(0)
    n_batch = pl.num_programs(0)
    t_len = q_ref.shape[1]
    nr = t_len * MOBA_QROWS
    chunk = MOBA_CHUNK
    k_chunks = 2 * n_blocks // chunk
    per_b = 2 * k_chunks
    total = n_batch * per_b

    def page_copy(src_hbm, bb, page, slot, j):
        return pltpu.make_async_copy(src_hbm.at[layer, pt_ref[bb, page]], buf_ref.at[slot, j],
                                     sem.at[slot])

    def start_chunk(c):
        bb = c // per_b
        i = c % per_b
        slot = c % MOBA_SLOTS

        @pl.when(i < k_chunks)
        def _():
            for j in range(chunk):
                page_copy(kt_hbm, bb, i * chunk + j, slot, j).start()

        @pl.when(i >= k_chunks)
        def _():
            for j in range(chunk):
                page_copy(vt_hbm, bb, (i - k_chunks) * chunk + j, slot, j).start()

    def next_chunk(c):
        @pl.when(c + MOBA_AHEAD < total)
        def _():
            start_chunk(c + MOBA_AHEAD)

        slot = c % MOBA_SLOTS
        for j in range(chunk):
            page_copy(kt_hbm, 0, 0, slot, j).wait()
        return slot

    @pl.when(b == 0)
    def _():
        for c in range(MOBA_AHEAD):
            start_chunk(c)

    qbd = _block_diag_q(q_ref[0], N_MIX_HEADS, MOBA_QROWS)
    qb = (qbd * ATTN_SCALE).astype(BF16)
    lane_w = _iota(ksum_ref.shape, 1)

    def k_body(i, carry):
        slot = next_chunk(b * per_b + i)
        for jj in range(chunk // 2):
            ka = buf_ref[slot, 2 * jj]
            kb = buf_ref[slot, 2 * jj + 1]
            page = i * chunk + 2 * jj
            sc_ref[page] = _dot(qb, ka.astype(BF16))
            sc_ref[page + 1] = _dot(qb, kb.astype(BF16))
            col = jnp.sum(ka + kb, axis=1, keepdims=True)
            ksum_ref[...] = jnp.where(lane_w == page // 2, col, ksum_ref[...])
        return carry

    ksum_ref[...] = jnp.zeros(ksum_ref.shape, F32)
    lax.fori_loop(0, k_chunks, k_body, 0)

    lane = _iota((nr, LANES), 1)
    gmat = _dot(qbd, ksum_ref[...], precision=HIGHEST) * (1.0 / MOBA_BLOCK)
    gmat = jnp.where(lane < n_blocks, gmat, NEG)
    cnt = jnp.zeros((nr, LANES), F32)
    for m in range(n_blocks):
        gm = jnp.broadcast_to(gmat[:, m:m + 1], (nr, LANES))
        beats = (gm > gmat) | ((gm == gmat) & (lane > m))
        cnt = cnt + jnp.where(beats, 1.0, 0.0)
    sel = jnp.where((cnt < MOBA_TOPK) & (lane < n_blocks), 1.0, 0.0)

    def block_ok(n):
        return jnp.broadcast_to(sel[:, n:n + 1], (nr, LANES)) > 0.5

    trow = _iota((nr, 1), 0) // MOBA_QROWS
    s_own = []
    for u in range(t_len):
        su = jnp.sum(qbd * kn_ref[0, u:u + 1, :], axis=1, keepdims=True) * ATTN_SCALE
        s_own.append(jnp.where(trow >= u, su, NEG))

    mx = jnp.full((nr, LANES), NEG, F32)
    for n in range(n_blocks):
        ok = block_ok(n)
        for half in range(2):
            mx = jnp.maximum(mx, jnp.where(ok, sc_ref[2 * n + half], NEG))
    m = jnp.max(mx, axis=1, keepdims=True)
    for su in s_own:
        m = jnp.maximum(m, su)
    lsum = jnp.zeros((nr, LANES), F32)
    for n in range(n_blocks):
        ok = block_ok(n)
        for half in range(2):
            p = jnp.exp(jnp.where(ok, sc_ref[2 * n + half], NEG) - m)
            sc_ref[2 * n + half] = p
            lsum = lsum + p
    l = jnp.sum(lsum, axis=1, keepdims=True)
    acc = jnp.zeros(acc_ref.shape, F32)
    for u in range(t_len):
        pu = jnp.exp(s_own[u] - m)
        l = l + pu
        acc = acc + pu * vn_ref[0, u:u + 1, :]
    acc_ref[...] = acc

    def v_body(i, carry):
        slot = next_chunk(b * per_b + k_chunks + i)
        for j in range(chunk):
            p = sc_ref[i * chunk + j].astype(BF16)
            acc_ref[...] += _dot_nt(p, buf_ref[slot, j].astype(BF16))
        return carry

    lax.fori_loop(0, k_chunks, v_body, 0)

    o = acc_ref[...] * (1.0 / l)
    for t, r in enumerate(_head_diag_rows(o, t_len, MOBA_QROWS)):
        o_ref[0, t:t + 1, :] = r


def _moba_sample(q3, kn3, vn3, ckt, cvt, layer, page_table):
    bd, t_len, w = q3.shape
    n_pages = page_table.shape[1]
    pages_per_block = MOBA_BLOCK // PAGE_SIZE
    assert pages_per_block == 2 and n_pages % pages_per_block == 0 and t_len <= MOBA_BLOCK
    nb = n_pages // pages_per_block
    assert n_pages % MOBA_CHUNK == 0 and MOBA_CHUNK % pages_per_block == 0 and nb <= LANES
    nr = t_len * MOBA_QROWS
    tok = pl.BlockSpec((1, t_len, w), lambda b, pt: (b, 0, 0))
    hbm = pl.BlockSpec(memory_space=pl.ANY)
    grid_spec = pltpu.PrefetchScalarGridSpec(
        num_scalar_prefetch=1,
        grid=(bd,),
        in_specs=[tok, tok, tok, hbm, hbm],
        out_specs=tok,
        scratch_shapes=[pltpu.VMEM((MOBA_SLOTS, MOBA_CHUNK, w, PAGE_SIZE), F32),
                        pltpu.SemaphoreType.DMA((MOBA_SLOTS,)),
                        pltpu.VMEM((n_pages, nr, PAGE_SIZE), F32),
                        pltpu.VMEM((w, LANES), F32),
                        pltpu.VMEM((nr, w), F32)])
    return pl.pallas_call(
        functools.partial(_moba_sample_kernel, layer=layer, n_blocks=nb),
        grid_spec=grid_spec,
        out_shape=jax.ShapeDtypeStruct((bd, t_len, w), F32),
        compiler_params=_cparams(1),
        name="moba_sample",
    )(page_table, q3, kn3, vn3, ckt, cvt)


def _dil_prompt_kernel(q0_ref, q1_ref, q2_ref, k0_ref, k1_ref, k2_ref, v0_ref, v1_ref, v2_ref,
                       o_ref, kb_ref, vt_ref, og_ref, lse_ref):
    qi = pl.program_id(2)
    tq = q0_ref.shape[1]
    s_len = k0_ref.shape[1]
    q_refs = (q0_ref, q1_ref, q2_ref)

    @pl.when(qi == 0)
    def _():
        for g, (k_ref, v_ref) in enumerate(((k0_ref, v0_ref), (k1_ref, v1_ref), (k2_ref, v2_ref))):
            kb_ref[g] = k_ref[0].astype(BF16)
            for n in range(s_len // LANES):
                vt_ref[g, n] = v_ref[0, n * LANES:(n + 1) * LANES, :].T.astype(BF16)

    lane = _iota((tq, LANES), 1)
    t0 = qi * tq

    def group(g, k0, n_keys, dist0):
        window, dil = SWA_GROUPS[g]
        d = dist0 + _iota((n_keys, tq), 1) - _iota((n_keys, tq), 0)
        ok = (d >= 0) & (d <= window)
        if dil > 1:
            ok = ok & ((d & (dil - 1)) == 0)
        okf = jnp.where(ok, 1.0, 0.0)
        ok2 = jnp.concatenate([okf, okf], axis=1) > 0.5
        kb = kb_ref[g, pl.ds(k0, n_keys), :]
        vt = jnp.concatenate([vt_ref[g, k0 // LANES + i] for i in range(n_keys // LANES)], axis=1)
        qb2 = jnp.concatenate(
            [jnp.where(lane // HEAD_DIM == e, q_refs[g][0] * Q_SCALE_LOG2, 0.0).astype(BF16)
             for e in range(2)], axis=0)
        m, l, acc = _attend(qb2, kb, vt, ok2)
        return acc / l, m + jnp.log2(l)

    os = [[], []]
    lses = [[], []]
    for g, (window, dil) in enumerate(SWA_GROUPS):
        if window + tq < s_len:
            n_keys = window + tq
            k0 = pl.multiple_of(jnp.maximum(t0 + tq - n_keys, 0), LANES)
            o2, lse2 = group(g, k0, n_keys, t0 - k0)
        else:
            for qv in range(s_len // tq):
                @pl.when(qi == qv)
                def _(qv=qv, g=g, window=window):
                    end = (qv + 1) * tq
                    n_keys = min(window + tq, end)
                    o2, lse2 = group(g, end - n_keys, n_keys, qv * tq - (end - n_keys))
                    og_ref[...] = o2
                    lse_ref[0:1, :] = lse2
            o2 = og_ref[...]
            lse2 = lse_ref[0:1, :]
        for e in range(2):
            os[e].append(o2[:, e * tq:(e + 1) * tq])
            lses[e].append(lse2[:, e * tq:(e + 1) * tq])
    outs = []
    for e in range(2):
        mx = jnp.maximum(jnp.maximum(lses[e][0], lses[e][1]), lses[e][2])
        ws = [jnp.exp2(x - mx) for x in lses[e]]
        tot = ws[0] + ws[1] + ws[2]
        outs.append((ws[0] * os[e][0] + ws[1] * os[e][1] + ws[2] * os[e][2]) / tot)
    feat = _iota((LANES, tq), 0)
    o_ref[0] = jnp.where(feat < HEAD_DIM, outs[0], outs[1]).T


def _dil_prompt(q3, k3, v3, tq):
    b, s, w = q3.shape
    n_pairs = SWA_WIDTH // LANES
    assert w == len(SWA_GROUPS) * SWA_WIDTH and s % tq == 0
    assert all(d & (d - 1) == 0 and wd % LANES == 0 for wd, d in SWA_GROUPS) and tq % LANES == 0

    def q_spec(g):
        return pl.BlockSpec((1, tq, LANES), lambda i, p, j: (i, j, g * n_pairs + p))

    def kv_spec(g):
        return pl.BlockSpec((1, s, LANES), lambda i, p, j: (i, 0, g * n_pairs + p))

    n_g = len(SWA_GROUPS)
    return pl.pallas_call(
        _dil_prompt_kernel,
        grid=(b, n_pairs, s // tq),
        in_specs=([q_spec(g) for g in range(n_g)] + [kv_spec(g) for g in range(n_g)]
                  + [kv_spec(g) for g in range(n_g)]),
        out_specs=pl.BlockSpec((1, tq, LANES), lambda i, p, j: (i, j, p)),
        out_shape=jax.ShapeDtypeStruct((b, s, SWA_WIDTH), F32),
        scratch_shapes=[pltpu.VMEM((n_g, s, LANES), BF16),
                        pltpu.VMEM((n_g, s // LANES, LANES, LANES), BF16),
                        pltpu.VMEM((LANES, 2 * tq), F32),
                        pltpu.VMEM((SUBLANES, 2 * tq), F32)],
        compiler_params=_cparams(3),
        name="dil_prompt",
    )(q3, q3, q3, k3, k3, k3, v3, v3, v3)


SWA_QROWS = 8


def _dil_sample_kernel(q_ref, kn_ref, vn_ref, b0_ref, b1_ref, b2_ref, o_ref):
    t_len = q_ref.shape[1]
    nr = t_len * SWA_QROWS
    q = q_ref[0]
    kn = kn_ref[0]
    vn = vn_ref[0]
    trow = _iota((nr, 1), 0) // SWA_QROWS
    os, lses = [], []
    for g, ((window, dil), buf_ref) in enumerate(zip(SWA_GROUPS, (b0_ref, b1_ref, b2_ref))):
        sl = slice(g * SWA_WIDTH, (g + 1) * SWA_WIDTH)
        qbd = _block_diag_q(q[:, sl], SWA_HEADS, SWA_QROWS)
        kt = buf_ref[0, 0].astype(BF16)
        vt = buf_ref[0, 1].astype(BF16)
        s = _dot((qbd * ATTN_SCALE).astype(BF16), kt)
        d = window + (_iota((nr, window), 0) // SWA_QROWS) - _iota((nr, window), 1)
        s = jnp.where((d <= window) & ((d & (dil - 1)) == 0), s, NEG)
        s_new = []
        for u in range(t_len):
            su = jnp.sum(qbd * kn[u:u + 1, sl], axis=1, keepdims=True) * ATTN_SCALE
            du = trow - u
            s_new.append(jnp.where((du >= 0) & ((du & (dil - 1)) == 0), su, NEG))
        m = jnp.max(s, axis=1, keepdims=True)
        for su in s_new:
            m = jnp.maximum(m, su)
        p = jnp.exp(s - m)
        l = jnp.sum(p, axis=1, keepdims=True)
        o = _dot_nt(p.astype(BF16), vt)
        for u in range(t_len):
            pu = jnp.exp(s_new[u] - m)
            l = l + pu
            o = o + pu * vn[u:u + 1, sl]
        os.append(o / l)
        lses.append(m + jnp.log(l))
    mx = jnp.maximum(jnp.maximum(lses[0], lses[1]), lses[2])
    ws = [jnp.exp(x - mx) for x in lses]
    tot = ws[0] + ws[1] + ws[2]
    out = (ws[0] * os[0] + ws[1] * os[1] + ws[2] * os[2]) / tot
    for t, r in enumerate(_head_diag_rows(out, t_len, SWA_QROWS)):
        o_ref[0, t:t + 1, :] = r


def _dil_sample(q3, kn3, vn3, bufs_t, layer):
    bd, t_len, w = q3.shape
    tok = pl.BlockSpec((1, t_len, w), lambda b: (b, 0, 0))
    buf_specs = []
    for (window, _), buf in zip(SWA_GROUPS, bufs_t):
        assert buf.shape[1:] == (bd, 2, SWA_WIDTH, window)
        buf_specs.append(pl.BlockSpec((None, 1, 2, SWA_WIDTH, window),
                                      lambda b: (layer, b, 0, 0, 0)))
    return pl.pallas_call(
        _dil_sample_kernel,
        grid=(bd,),
        in_specs=[tok, tok, tok] + buf_specs,
        out_specs=pl.BlockSpec((1, t_len, SWA_WIDTH), lambda b: (b, 0, 0)),
        out_shape=jax.ShapeDtypeStruct((bd, t_len, SWA_WIDTH), F32),
        compiler_params=_cparams(1),
        name="dil_sample",
    )(q3, kn3, vn3, *bufs_t)


def _ln_swish(y, lg, lb):
    mu = jnp.mean(y, axis=-1, keepdims=True)
    yc = y - mu
    var = jnp.mean(yc * yc, axis=-1, keepdims=True)
    z = yc * lax.rsqrt(var + LN_EPS) * lg + lb
    return z * _sigmoid(z)


def _conv_prompt_kernel(a_ref, g_ref, w_ref, b_ref, lg_ref, lb_ref, o_ref, tail_ref, ctx_ref,
                        sh_ref):
    qi = pl.program_id(1)
    tq = a_ref.shape[1]
    halo = CONV_HALO
    sub = SUBLANES

    @pl.when(qi == 0)
    def _():
        ctx_ref[0:halo, :] = jnp.zeros((halo, ctx_ref.shape[1]), F32)

    ctx_ref[halo:halo + tq, :] = a_ref[0] * _sigmoid(g_ref[0])
    n_sh = sh_ref.shape[1]
    for r in range(1, sub):
        sh_ref[r - 1] = ctx_ref[r:r + n_sh, :]
    first = halo - (CONV_WIDTH - 1)
    acc = jnp.zeros((tq, ctx_ref.shape[1]), F32)
    for j in range(CONV_WIDTH):
        start, phase = (first + j) // sub * sub, (first + j) % sub
        src = ctx_ref[start:start + tq, :] if phase == 0 else sh_ref[phase - 1, start:start + tq, :]
        acc = acc + src * w_ref[j:j + 1, :]
    o_ref[0] = _ln_swish(acc + b_ref[...], lg_ref[...], lb_ref[...])
    tail = ctx_ref[tq:tq + halo, :]
    ctx_ref[0:halo, :] = tail
    tail_ref[0] = tail


def _conv_prompt(a3, g3, w, b, lg, lb, tq):
    bsz, s, c = a3.shape
    assert s % tq == 0 and tq >= CONV_HALO and CONV_HALO % SUBLANES == 0
    tile = pl.BlockSpec((1, tq, c), lambda i, j: (i, j, 0))
    vec = pl.BlockSpec((1, c), lambda i, j: (0, 0))
    return pl.pallas_call(
        _conv_prompt_kernel,
        grid=(bsz, s // tq),
        in_specs=[tile, tile, pl.BlockSpec((CONV_WIDTH, c), lambda i, j: (0, 0)), vec, vec, vec],
        out_specs=[tile, pl.BlockSpec((1, CONV_HALO, c), lambda i, j: (i, 0, 0))],
        out_shape=[jax.ShapeDtypeStruct((bsz, s, c), F32),
                   jax.ShapeDtypeStruct((bsz, CONV_HALO, c), F32)],
        scratch_shapes=[pltpu.VMEM((CONV_HALO + tq, c), F32),
                        pltpu.VMEM((SUBLANES - 1, CONV_HALO - SUBLANES + tq, c), F32)],
        compiler_params=_cparams(2),
        name="conv_prompt",
    )(a3, g3, w, b.reshape(1, c), lg.reshape(1, c), lb.reshape(1, c))


def _conv_sample_kernel(st_ref, a_ref, g_ref, w_ref, b_ref, lg_ref, lb_ref, o_ref, ns_ref):
    n_hist = st_ref.shape[0]
    t_len = a_ref.shape[0]
    u = [a_ref[t] * _sigmoid(g_ref[t]) for t in range(t_len)]

    def ctx(i):
        return st_ref[i] if i < n_hist else u[i - n_hist]

    for t in range(t_len):
        acc = jnp.zeros(u[0].shape, F32)
        for j in range(CONV_WIDTH):
            acc = acc + ctx(t + j) * w_ref[j:j + 1, :]
        o_ref[t] = _ln_swish(acc + b_ref[...], lg_ref[...], lb_ref[...])
    for i in range(n_hist):
        ns_ref[i] = ctx(i + t_len)


def _conv_sample(state_t, a_t, g_t, w, b, lg, lb, layer):
    _, n_hist, bd, c = state_t.shape
    t_len = a_t.shape[0]
    assert n_hist == CONV_WIDTH - 1
    tok = pl.BlockSpec((t_len, bd, c), lambda i: (0, 0, 0))
    vec = pl.BlockSpec((1, c), lambda i: (0, 0))
    return pl.pallas_call(
        _conv_sample_kernel,
        grid=(1,),
        in_specs=[pl.BlockSpec((None, n_hist, bd, c), lambda i: (layer, 0, 0, 0)), tok, tok,
                  pl.BlockSpec((CONV_WIDTH, c), lambda i: (0, 0)), vec, vec, vec],
        out_specs=[tok, pl.BlockSpec((n_hist, bd, c), lambda i: (0, 0, 0))],
        out_shape=[jax.ShapeDtypeStruct((t_len, bd, c), F32),
                   jax.ShapeDtypeStruct((n_hist, bd, c), F32)],
        compiler_params=_cparams(1),
        name="conv_sample",
    )(state_t, a_t, g_t, w, b.reshape(1, c), lg.reshape(1, c), lb.reshape(1, c))


def kernel(x_prompt, x_sample, mem_prompt, cache_moba_k, cache_moba_v, page_table, state_swa_w128, state_swa_w512, state_swa_w2048, state_conv, cache_mem_k, cache_mem_v, g_mix, g_mem, w_mem_kv, w_in_a, w_out_a, w_in_b, w_out_b, w_in_c, conv_w, conv_b, conv_ln_g, conv_ln_b, w_out_c, g_ffn, w_ffn_up, w_ffn_down, g_final):
    b, s, d = x_prompt.shape
    bd, t_len, _ = x_sample.shape
    depth = g_mix.shape[0]
    kinds = tuple(i % N_MIXERS for i in range(depth))
    past_len = page_table.shape[1] * PAGE_SIZE
    n_mem = mem_prompt.shape[1]
    bm = 512
    bf = lambda w: w.astype(BF16)

    xp = x_prompt.reshape(b * s, d)
    xs = x_sample.reshape(bd * t_len, d)

    mem_kt, mem_vt = _mem_kv(mem_prompt, g_mem, bf(w_mem_kv))
    cmem_kt = cache_mem_k.transpose(0, 1, 3, 4, 2).reshape(depth, bd, MEM_WIDTH, n_mem)
    cmem_vt = cache_mem_v.transpose(0, 1, 3, 4, 2).reshape(depth, bd, MEM_WIDTH, n_mem)
    n_pool = cache_moba_k.shape[1]
    ckt = cache_moba_k.transpose(0, 1, 3, 4, 2).reshape(-1, n_pool, MIX_WIDTH, PAGE_SIZE)
    cvt = cache_moba_v.transpose(0, 1, 3, 4, 2).reshape(-1, n_pool, MIX_WIDTH, PAGE_SIZE)
    swa_states = (state_swa_w128, state_swa_w512, state_swa_w2048)
    swa_t = [st.transpose(0, 1, 3, 4, 5, 2).reshape(st.shape[0], bd, 2, SWA_WIDTH, st.shape[2])
             for st in swa_states]
    conv_t = state_conv.transpose(0, 2, 1, 3)

    moba_kp, moba_vp, moba_ks, moba_vs = [], [], [], []
    swa_p = [[] for _ in SWA_GROUPS]
    swa_s = [[] for _ in SWA_GROUPS]
    conv_p, conv_s = [], []

    for l in range(depth):
        kind = kinds[l]
        j = kinds[:l].count(kind)
        if kind == 2:
            w_in = bf(w_in_c[j])
            splits = (MIX_WIDTH, MIX_WIDTH, MEM_WIDTH)
            ap, gp, mqp = _in_proj(xp, g_mix[l], w_in, splits, bm)
            a_s, g_s, mqs = _in_proj(xs, g_mix[l], w_in, splits, bd * t_len)
            mix_p, tail = _conv_prompt(ap.reshape(b, s, -1), gp.reshape(b, s, -1), conv_w[j],
                                       conv_b[j], conv_ln_g[j], conv_ln_b[j], 256)
            mix_p = mix_p.reshape(b * s, -1)
            tb = lambda z: z.reshape(bd, t_len, -1).transpose(1, 0, 2)
            mix_s, new_state = _conv_sample(conv_t, tb(a_s), tb(g_s), conv_w[j], conv_b[j],
                                            conv_ln_g[j], conv_ln_b[j], j)
            mix_s = mix_s.transpose(1, 0, 2).reshape(bd * t_len, -1)
            conv_p.append(tail[:, CONV_HALO - (CONV_WIDTH - 1):])
            conv_s.append(new_state.transpose(1, 0, 2))
            w_out = bf(w_out_c[j])
        else:
            w_in = bf(w_in_a[j] if kind == 0 else w_in_b[j])
            splits = (MIX_WIDTH, MIX_WIDTH, MIX_WIDTH, MEM_WIDTH)
            qp, kp, vp, mqp = _in_proj(xp, g_mix[l], w_in, splits, bm)
            qs, ks, vs, mqs = _in_proj(xs, g_mix[l], w_in, splits, bd * t_len)
            qp3, kp3, vp3 = (z.reshape(b, s, MIX_WIDTH) for z in (qp, kp, vp))
            qs3, ks3, vs3 = (z.reshape(bd, t_len, MIX_WIDTH) for z in (qs, ks, vs))
            if kind == 0:
                mix_p = _moba_prompt(qp3, kp3, vp3)
                mix_s = _moba_sample(qs3, ks3, vs3, ckt, cvt, j, page_table)
                moba_kp.append(kp.reshape(b, s, N_MIX_HEADS, HEAD_DIM))
                moba_vp.append(vp.reshape(b, s, N_MIX_HEADS, HEAD_DIM))
                moba_ks.append(ks.reshape(bd, t_len, N_MIX_HEADS, HEAD_DIM))
                moba_vs.append(vs.reshape(bd, t_len, N_MIX_HEADS, HEAD_DIM))
                w_out = bf(w_out_a[j])
            else:
                mix_p = _dil_prompt(qp3, kp3, vp3, 256)
                mix_s = _dil_sample(qs3, ks3, vs3, swa_t, j)
                for g, (window, _) in enumerate(SWA_GROUPS):
                    sl = slice(g * SWA_WIDTH, (g + 1) * SWA_WIDTH)
                    heads = lambda z: z.reshape(z.shape[0], z.shape[1], SWA_HEADS, HEAD_DIM)
                    kv_p = jnp.stack([heads(kp3[:, :, sl]), heads(vp3[:, :, sl])], axis=2)
                    swa_p[g].append(kv_p[:, s - min(window, s):])
                    kv_s = jnp.stack([heads(ks3[:, :, sl]), heads(vs3[:, :, sl])], axis=2)
                    assert swa_states[g].shape[2] == window
                    full = jnp.concatenate([swa_states[g][j], kv_s], axis=1)
                    swa_s[g].append(full[:, full.shape[1] - min(window, past_len + t_len):])
                w_out = bf(w_out_b[j])
            mix_p = mix_p.reshape(b * s, -1)
            mix_s = mix_s.reshape(bd * t_len, -1)
        mem_p = _mem_attn(mqp.reshape(b, s, MEM_WIDTH), mem_kt, mem_vt, l, 512)
        mem_s = _mem_attn(mqs.reshape(bd, t_len, MEM_WIDTH), cmem_kt, cmem_vt, l, t_len)
        final = l == depth - 1
        wup, wdn = bf(w_ffn_up[l]), bf(w_ffn_down[l])
        xp = _out_ffn(xp, mix_p, mem_p.reshape(b * s, MEM_WIDTH), w_out, g_ffn[l], wup, wdn,
                      g_final, final, bm)
        xs = _out_ffn(xs, mix_s, mem_s.reshape(bd * t_len, MEM_WIDTH), w_out, g_ffn[l], wup, wdn,
                      g_final, final, bd * t_len)

    heads_t = lambda zt: zt.reshape(depth, b, N_MEM_HEADS, HEAD_DIM, n_mem).transpose(0, 1, 4, 2, 3)
    return (xp.reshape(b, s, d), xs.reshape(bd, t_len, d),
            jnp.stack(moba_kp), jnp.stack(moba_vp), jnp.stack(moba_ks), jnp.stack(moba_vs),
            jnp.stack(swa_p[0]), jnp.stack(swa_p[1]), jnp.stack(swa_p[2]),
            jnp.stack(swa_s[0]), jnp.stack(swa_s[1]), jnp.stack(swa_s[2]),
            jnp.stack(conv_p), jnp.stack(conv_s),
            heads_t(mem_kt), heads_t(mem_vt))
```

```python
import functools

import jax
import jax.numpy as jnp
from jax import lax
from jax.experimental import pallas as pl
from jax.experimental.pallas import tpu as pltpu

F32 = jnp.float32
BF16 = jnp.bfloat16

HEAD_DIM = 64
N_MIX_HEADS = 12
MIX_WIDTH = N_MIX_HEADS * HEAD_DIM
N_MEM_HEADS = 4
MEM_WIDTH = N_MEM_HEADS * HEAD_DIM
MOBA_BLOCK = 256
MOBA_TOPK = 3
PAGE_SIZE = 128
SWA_GROUPS = ((128, 1), (512, 4), (2048, 16))
SWA_HEADS = 4
SWA_WIDTH = SWA_HEADS * HEAD_DIM
CONV_WIDTH = 31
CONV_HALO = 32
SUBLANES = 8
N_MIXERS = 3
ATTN_SCALE = HEAD_DIM ** -0.5
NEG = -1e30
RMS_EPS = 1e-6
LN_EPS = 1e-5

LANES = 128
VMEM_LIMIT = 48 * 1024 * 1024
NT_DIMS = (((1,), (1,)), ((), ()))
HIGHEST = lax.Precision.HIGHEST


def _cparams(n_grid):
    return pltpu.CompilerParams(dimension_semantics=("arbitrary",) * n_grid,
                                vmem_limit_bytes=VMEM_LIMIT)


def _rms(x, g):
    return x * lax.rsqrt(jnp.mean(x * x, axis=-1, keepdims=True) + RMS_EPS) * g


def _dot(a, b, precision=None):
    return jnp.dot(a, b, precision=precision, preferred_element_type=F32)


def _dot_nt(a, b, precision=None):
    return lax.dot_general(a, b, NT_DIMS, precision=precision, preferred_element_type=F32)


def _sigmoid(x):
    return 1.0 / (1.0 + jnp.exp(-x))


def _iota(shape, axis):
    return lax.broadcasted_iota(jnp.int32, shape, axis)


def _in_proj_kernel(x_ref, g_ref, w_ref, *out_refs, splits):
    h = _rms(x_ref[...], g_ref[...]).astype(BF16)
    off = 0
    for o_ref, n in zip(out_refs, splits):
        o_ref[...] = _dot(h, w_ref[:, off:off + n])
        off += n


def _in_proj(x, g, w_bf16, splits, bm):
    m, d = x.shape
    n = w_bf16.shape[1]
    assert sum(splits) == n and m % bm == 0
    return pl.pallas_call(
        functools.partial(_in_proj_kernel, splits=splits),
        grid=(m // bm,),
        in_specs=[pl.BlockSpec((bm, d), lambda i: (i, 0)),
                  pl.BlockSpec((1, d), lambda i: (0, 0)),
                  pl.BlockSpec((d, n), lambda i: (0, 0), pipeline_mode=pl.Buffered(1))],
        out_specs=[pl.BlockSpec((bm, s), lambda i: (i, 0)) for s in splits],
        out_shape=[jax.ShapeDtypeStruct((m, s), F32) for s in splits],
        compiler_params=_cparams(1),
        name="in_proj",
    )(x, g.reshape(1, d), w_bf16)


def _in_proj_kv_kernel(x_ref, g_ref, w_ref, *refs):
    q_ref, k_ref, mq_ref, kt_ref, vt_ref = refs[-5:]
    h = _rms(x_ref[...], g_ref[...]).astype(BF16)
    q_ref[...] = _dot(h, w_ref[:, 0:MIX_WIDTH])
    k = _dot(h, w_ref[:, MIX_WIDTH:2 * MIX_WIDTH])
    k_ref[...] = k
    kt_ref[0, 0] = k.T
    vt_ref[0, 0] = _dot(h, w_ref[:, 2 * MIX_WIDTH:3 * MIX_WIDTH]).T
    mq_ref[...] = _dot(h, w_ref[:, 3 * MIX_WIDTH:])


def _in_proj_kv(x, g, w_bf16, kv_t, layer, n_layers, batch, bm):
    m, d = x.shape
    n = w_bf16.shape[1]
    s = m // batch
    tiles = s // bm
    assert n == 3 * MIX_WIDTH + MEM_WIDTH and m == batch * s and s % bm == 0
    rows = lambda w: pl.BlockSpec((bm, w), lambda i: (i, 0))
    t_spec = pl.BlockSpec((1, 1, MIX_WIDTH, bm), lambda i: (layer, i // tiles, 0, i % tiles))
    t_shape = jax.ShapeDtypeStruct((n_layers, batch, MIX_WIDTH, s), F32)
    in_specs = [rows(d), pl.BlockSpec((1, d), lambda i: (0, 0)),
                pl.BlockSpec((d, n), lambda i: (0, 0), pipeline_mode=pl.Buffered(1))]
    args = [x, g.reshape(1, d), w_bf16]
    aliases = {}
    if kv_t is not None:
        in_specs += [pl.BlockSpec(memory_space=pl.ANY)] * 2
        args += list(kv_t)
        aliases = {3: 3, 4: 4}
    return pl.pallas_call(
        _in_proj_kv_kernel,
        grid=(m // bm,),
        in_specs=in_specs,
        out_specs=[rows(MIX_WIDTH), rows(MIX_WIDTH), rows(MEM_WIDTH), t_spec, t_spec],
        out_shape=[jax.ShapeDtypeStruct((m, MIX_WIDTH), F32), jax.ShapeDtypeStruct((m, MIX_WIDTH), F32),
                   jax.ShapeDtypeStruct((m, MEM_WIDTH), F32), t_shape, t_shape],
        input_output_aliases=aliases,
        compiler_params=_cparams(1),
        name="in_proj_kv",
    )(*args)


def _out_ffn_kernel(x_ref, mix_ref, mem_ref, wo_mix_ref, wo_mem_ref, g_ref, wup_ref, wdn_ref,
                    gfin_ref, o_ref, *, ff_chunk, final):
    x = (x_ref[...] + _dot(mix_ref[...].astype(BF16), wo_mix_ref[...])
         + _dot(mem_ref[...].astype(BF16), wo_mem_ref[...]))
    h = _rms(x, g_ref[...]).astype(BF16)
    o_ref[...] = x
    d_ff = wup_ref.shape[1]
    for c in range(d_ff // ff_chunk):
        u = _dot(h, wup_ref[:, c * ff_chunk:(c + 1) * ff_chunk])
        a = jnp.square(jnp.maximum(u, 0.0)).astype(BF16)
        o_ref[...] += _dot(a, wdn_ref[c * ff_chunk:(c + 1) * ff_chunk, :])
    if final:
        o_ref[...] = _rms(o_ref[...], gfin_ref[...])


def _out_ffn(x, mix, mem, wo_bf16, g, wup_bf16, wdn_bf16, g_final, final, bm):
    m, d = x.shape
    wmix = mix.shape[1]
    wmem = mem.shape[1]
    d_ff = wup_bf16.shape[1]
    const = lambda i: (0, 0)
    single = pl.Buffered(1)
    return pl.pallas_call(
        functools.partial(_out_ffn_kernel, ff_chunk=1024, final=final),
        grid=(m // bm,),
        in_specs=[pl.BlockSpec((bm, d), lambda i: (i, 0)),
                  pl.BlockSpec((bm, wmix), lambda i: (i, 0)),
                  pl.BlockSpec((bm, wmem), lambda i: (i, 0)),
                  pl.BlockSpec((wmix, d), const, pipeline_mode=single),
                  pl.BlockSpec((wmem, d), const, pipeline_mode=single),
                  pl.BlockSpec((1, d), const),
                  pl.BlockSpec((d, d_ff), const, pipeline_mode=single),
                  pl.BlockSpec((d_ff, d), const, pipeline_mode=single),
                  pl.BlockSpec((1, d), const)],
        out_specs=pl.BlockSpec((bm, d), lambda i: (i, 0)),
        out_shape=jax.ShapeDtypeStruct((m, d), F32),
        compiler_params=_cparams(1),
        name="out_ffn",
    )(x, mix, mem, wo_bf16[:wmix], wo_bf16[wmix:], g.reshape(1, d), wup_bf16, wdn_bf16,
      g_final.reshape(1, d))


def _mem_kv_kernel(mem_ref, g_ref, w_ref, kt_ref, vt_ref):
    h = _rms(mem_ref[0], g_ref[0]).astype(BF16)
    zt = _dot(h, w_ref[0]).T
    kt_ref[0, 0] = zt[:MEM_WIDTH]
    vt_ref[0, 0] = zt[MEM_WIDTH:]


def _mem_kv(mem, g_mem, w_bf16):
    b, n_mem, d = mem.shape
    depth = w_bf16.shape[0]
    out = jax.ShapeDtypeStruct((depth, b, MEM_WIDTH, n_mem), F32)
    return pl.pallas_call(
        _mem_kv_kernel,
        grid=(depth, b),
        in_specs=[pl.BlockSpec((1, n_mem, d), lambda l, i: (i, 0, 0)),
                  pl.BlockSpec((1, 1, d), lambda l, i: (l, 0, 0)),
                  pl.BlockSpec((1, d, 2 * MEM_WIDTH), lambda l, i: (l, 0, 0))],
        out_specs=[pl.BlockSpec((1, 1, MEM_WIDTH, n_mem), lambda l, i: (l, i, 0, 0))] * 2,
        out_shape=[out, out],
        compiler_params=_cparams(2),
        name="mem_kv",
    )(mem, g_mem.reshape(depth, 1, d), w_bf16)


def _mem_attn_kernel(q_ref, kt_ref, vt_ref, o_ref):
    for i in range(q_ref.shape[0]):
        q = q_ref[i]
        kt = kt_ref[i].astype(BF16)
        vt = vt_ref[i].astype(BF16)
        lane_head = _iota(q.shape, 1) // HEAD_DIM
        out = jnp.zeros(q.shape, F32)
        for h in range(N_MEM_HEADS):
            qh = jnp.where(lane_head == h, q * ATTN_SCALE, 0.0).astype(BF16)
            s = _dot(qh, kt)
            p = jnp.exp(s - jnp.max(s, axis=-1, keepdims=True))
            l = jnp.sum(p, axis=-1, keepdims=True)
            o = _dot_nt(p.astype(BF16), vt)
            out = jnp.where(lane_head == h, o / l, out)
        o_ref[i] = out


def _mem_attn(q3, kt_all, vt_all, layer, tq, bb=1):
    nb, s, w = q3.shape
    n_mem = kt_all.shape[-1]
    assert nb % bb == 0 and s % tq == 0
    kv_spec = pl.BlockSpec((None, bb, w, n_mem), lambda b, i: (layer, b, 0, 0))
    return pl.pallas_call(
        _mem_attn_kernel,
        grid=(nb // bb, s // tq),
        in_specs=[pl.BlockSpec((bb, tq, w), lambda b, i: (b, i, 0)), kv_spec, kv_spec],
        out_specs=pl.BlockSpec((bb, tq, w), lambda b, i: (b, i, 0)),
        out_shape=jax.ShapeDtypeStruct((nb, s, w), F32),
        compiler_params=_cparams(2),
        name="mem_attn",
    )(q3, kt_all, vt_all)


LOG2E = 1.4426950408889634
Q_SCALE_LOG2 = ATTN_SCALE * LOG2E


def _attend(qb, kb, vt, ok):
    s = jnp.where(ok, _dot_nt(kb, qb), NEG)
    m = jnp.max(s, axis=0, keepdims=True)
    p = jnp.exp2(s - m)
    l = jnp.sum(p, axis=0, keepdims=True)
    acc = _dot(vt, p.astype(BF16))
    return m, l, acc


def _moba_prompt_kernel(q_ref, k_ref, vt_in_ref, o_ref, kmean_ref, kb_ref, vt_ref):
    qi = pl.program_id(2)
    blk = MOBA_BLOCK
    nb = k_ref.shape[1] // blk

    @pl.when(qi == 0)
    def _():
        for n in range(nb):
            rows = slice(n * blk, (n + 1) * blk)
            kblk = k_ref[0, rows, :]
            kmean_ref[n:n + 1, :] = jnp.mean(kblk, axis=0, keepdims=True)
            kb_ref[rows, :] = kblk.astype(BF16)
        vt_ref[...] = vt_in_ref[0].astype(BF16)

    q = q_ref[0]
    lane = _iota((blk, LANES), 1)
    lane_nb = _iota((nb, LANES), 1)
    blk_id = _iota((nb, blk), 0)
    kmean = kmean_ref[...]
    valid = blk_id < qi
    qbs, sels = [], []
    for e in range(2):
        qe = jnp.where(lane // HEAD_DIM == e, q, 0.0)
        kme = jnp.where(lane_nb // HEAD_DIM == e, kmean, 0.0)
        gate = jnp.where(valid, _dot_nt(kme, qe, precision=HIGHEST), NEG)
        cnt = jnp.zeros((nb, blk), F32)
        for m in range(nb):
            gm = gate[m:m + 1, :]
            beats = (gm > gate) | ((gm == gate) & (blk_id > m))
            cnt = cnt + jnp.where(beats, 1.0, 0.0)
        sels.append(jnp.where((cnt < MOBA_TOPK) & valid, 1.0, 0.0))
        qbs.append((qe * Q_SCALE_LOG2).astype(BF16))

    qb2 = jnp.concatenate(qbs, axis=0)
    sel2 = jnp.concatenate(sels, axis=1)
    causal = _iota((blk, blk), 0) <= _iota((blk, blk), 1)
    causal2 = jnp.where(jnp.concatenate([causal, causal], axis=1), 1.0, 0.0)
    feat = _iota((LANES, blk), 0)

    for n_past in range(nb):
        @pl.when(qi == n_past)
        def _(n_past=n_past):
            n_keys = (n_past + 1) * blk
            ok = jnp.concatenate([jnp.broadcast_to(sel2[n:n + 1, :], (blk, 2 * blk))
                                  for n in range(n_past)] + [causal2], axis=0) > 0.5
            _, l, acc = _attend(qb2, kb_ref[0:n_keys, :], vt_ref[:, 0:n_keys], ok)
            out = acc / l
            o_ref[0] = jnp.where(feat < HEAD_DIM, out[:, :blk], out[:, blk:]).T


def _moba_prompt(q3, k3, vt_all, layer):
    b, s, w = q3.shape
    blk = MOBA_BLOCK
    nb = s // blk
    assert s % blk == 0 and w % LANES == 0 and vt_all.shape[1:] == (b, w, s)
    k_spec = pl.BlockSpec((1, s, LANES), lambda i, p, j: (i, 0, p))
    vt_spec = pl.BlockSpec((None, 1, LANES, s), lambda i, p, j: (layer, i, p, 0))
    return pl.pallas_call(
        _moba_prompt_kernel,
        grid=(b, w // LANES, nb),
        in_specs=[pl.BlockSpec((1, blk, LANES), lambda i, p, j: (i, j, p)), k_spec, vt_spec],
        out_specs=pl.BlockSpec((1, blk, LANES), lambda i, p, j: (i, j, p)),
        out_shape=jax.ShapeDtypeStruct((b, s, w), F32),
        scratch_shapes=[pltpu.VMEM((nb, LANES), F32),
                        pltpu.VMEM((s, LANES), BF16),
                        pltpu.VMEM((LANES, s), BF16)],
        compiler_params=_cparams(3),
        name="moba_prompt",
    )(q3, k3, vt_all)


def _block_diag_q(q, n_heads, rows_per_tok):
    t_len, w = q.shape
    hrow = _iota((rows_per_tok, w), 0)
    lane_head = _iota((rows_per_tok, w), 1) // HEAD_DIM
    keep = (hrow == lane_head) & (hrow < n_heads)
    parts = [jnp.where(keep, jnp.broadcast_to(q[t:t + 1, :], (rows_per_tok, w)), 0.0)
             for t in range(t_len)]
    return jnp.concatenate(parts, axis=0)


def _head_diag_rows(o, t_len, rows_per_tok):
    w = o.shape[1]
    hrow = _iota((rows_per_tok, w), 0)
    lane_head = _iota((rows_per_tok, w), 1) // HEAD_DIM
    keep = hrow == lane_head
    return [jnp.sum(jnp.where(keep, o[t * rows_per_tok:(t + 1) * rows_per_tok, :], 0.0), axis=0,
                    keepdims=True) for t in range(t_len)]


MOBA_QROWS = 16
MOBA_CHUNK = 8
MOBA_AHEAD = 3
MOBA_SLOTS = MOBA_AHEAD + 1


def _moba_sample_kernel(pt_ref, q_ref, kn_ref, vn_ref, kt_hbm, vt_hbm, o_ref,
                        buf_ref, sem, sc_ref, ksum_ref, acc_ref, *, layer, n_blocks):
    b = pl.program_id(0)
    n_batch = pl.num_programs(0)
    t_len = q_ref.shape[1]
    nr = t_len * MOBA_QROWS
    chunk = MOBA_CHUNK
    k_chunks = 2 * n_blocks // chunk
    per_b = 2 * k_chunks
    total = n_batch * per_b

    def page_copy(src_hbm, bb, page, slot, j):
        return pltpu.make_async_copy(src_hbm.at[layer, pt_ref[bb, page]], buf_ref.at[slot, j],
                                     sem.at[slot])

    def start_chunk(c):
        bb = c // per_b
        i = c % per_b
        slot = c % MOBA_SLOTS

        @pl.when(i < k_chunks)
        def _():
            for j in range(chunk):
                page_copy(kt_hbm, bb, i * chunk + j, slot, j).start()

        @pl.when(i >= k_chunks)
        def _():
            for j in range(chunk):
                page_copy(vt_hbm, bb, (i - k_chunks) * chunk + j, slot, j).start()

    def next_chunk(c):
        @pl.when(c + MOBA_AHEAD < total)
        def _():
            start_chunk(c + MOBA_AHEAD)

        slot = c % MOBA_SLOTS
        for j in range(chunk):
            page_copy(kt_hbm, 0, 0, slot, j).wait()
        return slot

    @pl.when(b == 0)
    def _():
        for c in range(MOBA_AHEAD):
            start_chunk(c)

    qbd = _block_diag_q(q_ref[0], N_MIX_HEADS, MOBA_QROWS)
    qb = (qbd * ATTN_SCALE).astype(BF16)
    lane_w = _iota(ksum_ref.shape, 1)

    def k_body(i, carry):
        slot = next_chunk(b * per_b + i)
        for jj in range(chunk // 2):
            ka = buf_ref[slot, 2 * jj]
            kb = buf_ref[slot, 2 * jj + 1]
            page = i * chunk + 2 * jj
            sc_ref[page] = _dot(qb, ka.astype(BF16))
            sc_ref[page + 1] = _dot(qb, kb.astype(BF16))
            col = jnp.sum(ka + kb, axis=1, keepdims=True)
            ksum_ref[...] = jnp.where(lane_w == page // 2, col, ksum_ref[...])
        return carry

    ksum_ref[...] = jnp.zeros(ksum_ref.shape, F32)
    lax.fori_loop(0, k_chunks, k_body, 0)

    lane = _iota((nr, LANES), 1)
    gmat = _dot(qbd, ksum_ref[...], precision=HIGHEST) * (1.0 / MOBA_BLOCK)
    gmat = jnp.where(lane < n_blocks, gmat, NEG)
    cnt = jnp.zeros((nr, LANES), F32)
    for m in range(n_blocks):
        gm = jnp.broadcast_to(gmat[:, m:m + 1], (nr, LANES))
        beats = (gm > gmat) | ((gm == gmat) & (lane > m))
        cnt = cnt + jnp.where(beats, 1.0, 0.0)
    sel = jnp.where((cnt < MOBA_TOPK) & (lane < n_blocks), 1.0, 0.0)

    def block_ok(n):
        return jnp.broadcast_to(sel[:, n:n + 1], (nr, LANES)) > 0.5

    trow = _iota((nr, 1), 0) // MOBA_QROWS
    s_own = []
    for u in range(t_len):
        su = jnp.sum(qbd * kn_ref[0, u:u + 1, :], axis=1, keepdims=True) * ATTN_SCALE
        s_own.append(jnp.where(trow >= u, su, NEG))

    mx = jnp.full((nr, LANES), NEG, F32)
    for n in range(n_blocks):
        ok = block_ok(n)
        for half in range(2):
            mx = jnp.maximum(mx, jnp.where(ok, sc_ref[2 * n + half], NEG))
    m = jnp.max(mx, axis=1, keepdims=True)
    for su in s_own:
        m = jnp.maximum(m, su)
    lsum = jnp.zeros((nr, LANES), F32)
    for n in range(n_blocks):
        ok = block_ok(n)
        for half in range(2):
            p = jnp.exp(jnp.where(ok, sc_ref[2 * n + half], NEG) - m)
            sc_ref[2 * n + half] = p
            lsum = lsum + p
    l = jnp.sum(lsum, axis=1, keepdims=True)
    acc = jnp.zeros(acc_ref.shape, F32)
    for u in range(t_len):
        pu = jnp.exp(s_own[u] - m)
        l = l + pu
        acc = acc + pu * vn_ref[0, u:u + 1, :]
    acc_ref[...] = acc

    def v_body(i, carry):
        slot = next_chunk(b * per_b + k_chunks + i)
        for j in range(chunk):
            p = sc_ref[i * chunk + j].astype(BF16)
            acc_ref[...] += _dot_nt(p, buf_ref[slot, j].astype(BF16))
        return carry

    lax.fori_loop(0, k_chunks, v_body, 0)

    o = acc_ref[...] * (1.0 / l)
    for t, r in enumerate(_head_diag_rows(o, t_len, MOBA_QROWS)):
        o_ref[0, t:t + 1, :] = r


def _moba_sample(q3, kn3, vn3, ckt, cvt, layer, page_table):
    bd, t_len, w = q3.shape
    n_pages = page_table.shape[1]
    pages_per_block = MOBA_BLOCK // PAGE_SIZE
    assert pages_per_block == 2 and n_pages % pages_per_block == 0 and t_len <= MOBA_BLOCK
    nb = n_pages // pages_per_block
    assert n_pages % MOBA_CHUNK == 0 and MOBA_CHUNK % pages_per_block == 0 and nb <= LANES
    nr = t_len * MOBA_QROWS
    tok = pl.BlockSpec((1, t_len, w), lambda b, pt: (b, 0, 0))
    hbm = pl.BlockSpec(memory_space=pl.ANY)
    grid_spec = pltpu.PrefetchScalarGridSpec(
        num_scalar_prefetch=1,
        grid=(bd,),
        in_specs=[tok, tok, tok, hbm, hbm],
        out_specs=tok,
        scratch_shapes=[pltpu.VMEM((MOBA_SLOTS, MOBA_CHUNK, w, PAGE_SIZE), F32),
                        pltpu.SemaphoreType.DMA((MOBA_SLOTS,)),
                        pltpu.VMEM((n_pages, nr, PAGE_SIZE), F32),
                        pltpu.VMEM((w, LANES), F32),
                        pltpu.VMEM((nr, w), F32)])
    return pl.pallas_call(
        functools.partial(_moba_sample_kernel, layer=layer, n_blocks=nb),
        grid_spec=grid_spec,
        out_shape=jax.ShapeDtypeStruct((bd, t_len, w), F32),
        compiler_params=_cparams(1),
        name="moba_sample",
    )(page_table, q3, kn3, vn3, ckt, cvt)


def _dil_prompt_kernel(q0_ref, q1_ref, q2_ref, k0_ref, k1_ref, k2_ref, v0_ref, v1_ref, v2_ref,
                       o_ref, kb_ref, vt_ref, og_ref, lse_ref):
    qi = pl.program_id(2)
    tq = q0_ref.shape[1]
    s_len = k0_ref.shape[1]
    q_refs = (q0_ref, q1_ref, q2_ref)

    @pl.when(qi == 0)
    def _():
        for g, (k_ref, v_ref) in enumerate(((k0_ref, v0_ref), (k1_ref, v1_ref), (k2_ref, v2_ref))):
            kb_ref[g] = k_ref[0].astype(BF16)
            for n in range(s_len // LANES):
                vt_ref[g, n] = v_ref[0, :, n * LANES:(n + 1) * LANES].astype(BF16)

    lane = _iota((tq, LANES), 1)
    t0 = qi * tq

    def group(g, k0, n_keys, dist0):
        window, dil = SWA_GROUPS[g]
        d = dist0 + _iota((n_keys, tq), 1) - _iota((n_keys, tq), 0)
        ok = (d >= 0) & (d <= window)
        if dil > 1:
            ok = ok & ((d & (dil - 1)) == 0)
        okf = jnp.where(ok, 1.0, 0.0)
        ok2 = jnp.concatenate([okf, okf], axis=1) > 0.5
        kb = kb_ref[g, pl.ds(k0, n_keys), :]
        vt = jnp.concatenate([vt_ref[g, k0 // LANES + i] for i in range(n_keys // LANES)], axis=1)
        qb2 = jnp.concatenate(
            [jnp.where(lane // HEAD_DIM == e, q_refs[g][0] * Q_SCALE_LOG2, 0.0).astype(BF16)
             for e in range(2)], axis=0)
        m, l, acc = _attend(qb2, kb, vt, ok2)
        return acc / l, m + jnp.log2(l)

    os = [[], []]
    lses = [[], []]
    for g, (window, dil) in enumerate(SWA_GROUPS):
        if window + tq < s_len:
            n_keys = window + tq
            k0 = pl.multiple_of(jnp.maximum(t0 + tq - n_keys, 0), LANES)
            o2, lse2 = group(g, k0, n_keys, t0 - k0)
        else:
            for qv in range(s_len // tq):
                @pl.when(qi == qv)
                def _(qv=qv, g=g, window=window):
                    end = (qv + 1) * tq
                    n_keys = min(window + tq, end)
                    o2, lse2 = group(g, end - n_keys, n_keys, qv * tq - (end - n_keys))
                    og_ref[...] = o2
                    lse_ref[0:1, :] = lse2
            o2 = og_ref[...]
            lse2 = lse_ref[0:1, :]
        for e in range(2):
            os[e].append(o2[:, e * tq:(e + 1) * tq])
            lses[e].append(lse2[:, e * tq:(e + 1) * tq])
    outs = []
    for e in range(2):
        mx = jnp.maximum(jnp.maximum(lses[e][0], lses[e][1]), lses[e][2])
        ws = [jnp.exp2(x - mx) for x in lses[e]]
        tot = ws[0] + ws[1] + ws[2]
        outs.append((ws[0] * os[e][0] + ws[1] * os[e][1] + ws[2] * os[e][2]) / tot)
    feat = _iota((LANES, tq), 0)
    o_ref[0] = jnp.where(feat < HEAD_DIM, outs[0], outs[1]).T


def _dil_prompt(q3, k3, vt_all, layer, tq):
    b, s, w = q3.shape
    n_pairs = SWA_WIDTH // LANES
    assert w == len(SWA_GROUPS) * SWA_WIDTH and s % tq == 0 and vt_all.shape[1:] == (b, w, s)
    assert all(d & (d - 1) == 0 and wd % LANES == 0 for wd, d in SWA_GROUPS) and tq % LANES == 0

    def q_spec(g):
        return pl.BlockSpec((1, tq, LANES), lambda i, p, j: (i, j, g * n_pairs + p))

    def kv_spec(g):
        return pl.BlockSpec((1, s, LANES), lambda i, p, j: (i, 0, g * n_pairs + p))

    def vt_spec(g):
        return pl.BlockSpec((None, 1, LANES, s), lambda i, p, j: (layer, i, g * n_pairs + p, 0))

    n_g = len(SWA_GROUPS)
    return pl.pallas_call(
        _dil_prompt_kernel,
        grid=(b, n_pairs, s // tq),
        in_specs=([q_spec(g) for g in range(n_g)] + [kv_spec(g) for g in range(n_g)]
                  + [vt_spec(g) for g in range(n_g)]),
        out_specs=pl.BlockSpec((1, tq, LANES), lambda i, p, j: (i, j, p)),
        out_shape=jax.ShapeDtypeStruct((b, s, SWA_WIDTH), F32),
        scratch_shapes=[pltpu.VMEM((n_g, s, LANES), BF16),
                        pltpu.VMEM((n_g, s // LANES, LANES, LANES), BF16),
                        pltpu.VMEM((LANES, 2 * tq), F32),
                        pltpu.VMEM((SUBLANES, 2 * tq), F32)],
        compiler_params=_cparams(3),
        name="dil_prompt",
    )(q3, q3, q3, k3, k3, k3, vt_all, vt_all, vt_all)


SWA_QROWS = 8


def _dil_sample_kernel(q_ref, kn_ref, vn_ref, b0_ref, b1_ref, b2_ref, o_ref):
    t_len = q_ref.shape[1]
    nr = t_len * SWA_QROWS
    q = q_ref[0]
    kn = kn_ref[0]
    vn = vn_ref[0]
    trow = _iota((nr, 1), 0) // SWA_QROWS
    os, lses = [], []
    for g, ((window, dil), buf_ref) in enumerate(zip(SWA_GROUPS, (b0_ref, b1_ref, b2_ref))):
        sl = slice(g * SWA_WIDTH, (g + 1) * SWA_WIDTH)
        qbd = _block_diag_q(q[:, sl], SWA_HEADS, SWA_QROWS)
        kt = buf_ref[0, 0].astype(BF16)
        vt = buf_ref[0, 1].astype(BF16)
        s = _dot((qbd * ATTN_SCALE).astype(BF16), kt)
        d = window + (_iota((nr, window), 0) // SWA_QROWS) - _iota((nr, window), 1)
        s = jnp.where((d <= window) & ((d & (dil - 1)) == 0), s, NEG)
        s_new = []
        for u in range(t_len):
            su = jnp.sum(qbd * kn[u:u + 1, sl], axis=1, keepdims=True) * ATTN_SCALE
            du = trow - u
            s_new.append(jnp.where((du >= 0) & ((du & (dil - 1)) == 0), su, NEG))
        m = jnp.max(s, axis=1, keepdims=True)
        for su in s_new:
            m = jnp.maximum(m, su)
        p = jnp.exp(s - m)
        l = jnp.sum(p, axis=1, keepdims=True)
        o = _dot_nt(p.astype(BF16), vt)
        for u in range(t_len):
            pu = jnp.exp(s_new[u] - m)
            l = l + pu
            o = o + pu * vn[u:u + 1, sl]
        os.append(o / l)
        lses.append(m + jnp.log(l))
    mx = jnp.maximum(jnp.maximum(lses[0], lses[1]), lses[2])
    ws = [jnp.exp(x - mx) for x in lses]
    tot = ws[0] + ws[1] + ws[2]
    out = (ws[0] * os[0] + ws[1] * os[1] + ws[2] * os[2]) / tot
    for t, r in enumerate(_head_diag_rows(out, t_len, SWA_QROWS)):
        o_ref[0, t:t + 1, :] = r


def _dil_sample(q3, kn3, vn3, bufs_t, layer):
    bd, t_len, w = q3.shape
    tok = pl.BlockSpec((1, t_len, w), lambda b: (b, 0, 0))
    buf_specs = []
    for (window, _), buf in zip(SWA_GROUPS, bufs_t):
        assert buf.shape[1:] == (bd, 2, SWA_WIDTH, window)
        buf_specs.append(pl.BlockSpec((None, 1, 2, SWA_WIDTH, window),
                                      lambda b: (layer, b, 0, 0, 0)))
    return pl.pallas_call(
        _dil_sample_kernel,
        grid=(bd,),
        in_specs=[tok, tok, tok] + buf_specs,
        out_specs=pl.BlockSpec((1, t_len, SWA_WIDTH), lambda b: (b, 0, 0)),
        out_shape=jax.ShapeDtypeStruct((bd, t_len, SWA_WIDTH), F32),
        compiler_params=_cparams(1),
        name="dil_sample",
    )(q3, kn3, vn3, *bufs_t)


def _ln_swish(y, lg, lb):
    mu = jnp.mean(y, axis=-1, keepdims=True)
    yc = y - mu
    var = jnp.mean(yc * yc, axis=-1, keepdims=True)
    z = yc * lax.rsqrt(var + LN_EPS) * lg + lb
    return z * _sigmoid(z)


def _conv_prompt_kernel(a_ref, g_ref, w_ref, b_ref, lg_ref, lb_ref, o_ref, tail_ref, ctx_ref,
                        sh_ref):
    qi = pl.program_id(1)
    tq = a_ref.shape[1]
    halo = CONV_HALO
    sub = SUBLANES

    @pl.when(qi == 0)
    def _():
        ctx_ref[0:halo, :] = jnp.zeros((halo, ctx_ref.shape[1]), F32)

    ctx_ref[halo:halo + tq, :] = a_ref[0] * _sigmoid(g_ref[0])
    n_sh = sh_ref.shape[1]
    for r in range(1, sub):
        sh_ref[r - 1] = ctx_ref[r:r + n_sh, :]
    first = halo - (CONV_WIDTH - 1)
    acc = jnp.zeros((tq, ctx_ref.shape[1]), F32)
    for j in range(CONV_WIDTH):
        start, phase = (first + j) // sub * sub, (first + j) % sub
        src = ctx_ref[start:start + tq, :] if phase == 0 else sh_ref[phase - 1, start:start + tq, :]
        acc = acc + src * w_ref[j:j + 1, :]
    o_ref[0] = _ln_swish(acc + b_ref[...], lg_ref[...], lb_ref[...])
    tail = ctx_ref[tq:tq + halo, :]
    ctx_ref[0:halo, :] = tail
    tail_ref[0] = tail


def _conv_prompt(a3, g3, w, b, lg, lb, tq):
    bsz, s, c = a3.shape
    assert s % tq == 0 and tq >= CONV_HALO and CONV_HALO % SUBLANES == 0
    tile = pl.BlockSpec((1, tq, c), lambda i, j: (i, j, 0))
    vec = pl.BlockSpec((1, c), lambda i, j: (0, 0))
    return pl.pallas_call(
        _conv_prompt_kernel,
        grid=(bsz, s // tq),
        in_specs=[tile, tile, pl.BlockSpec((CONV_WIDTH, c), lambda i, j: (0, 0)), vec, vec, vec],
        out_specs=[tile, pl.BlockSpec((1, CONV_HALO, c), lambda i, j: (i, 0, 0))],
        out_shape=[jax.ShapeDtypeStruct((bsz, s, c), F32),
                   jax.ShapeDtypeStruct((bsz, CONV_HALO, c), F32)],
        scratch_shapes=[pltpu.VMEM((CONV_HALO + tq, c), F32),
                        pltpu.VMEM((SUBLANES - 1, CONV_HALO - SUBLANES + tq, c), F32)],
        compiler_params=_cparams(2),
        name="conv_prompt",
    )(a3, g3, w, b.reshape(1, c), lg.reshape(1, c), lb.reshape(1, c))


def _conv_sample_kernel(st_ref, a_ref, g_ref, w_ref, b_ref, lg_ref, lb_ref, o_ref, ns_ref):
    n_hist = st_ref.shape[0]
    t_len = a_ref.shape[0]
    u = [a_ref[t] * _sigmoid(g_ref[t]) for t in range(t_len)]

    def ctx(i):
        return st_ref[i] if i < n_hist else u[i - n_hist]

    for t in range(t_len):
        acc = jnp.zeros(u[0].shape, F32)
        for j in range(CONV_WIDTH):
            acc = acc + ctx(t + j) * w_ref[j:j + 1, :]
        o_ref[t] = _ln_swish(acc + b_ref[...], lg_ref[...], lb_ref[...])
    for i in range(n_hist):
        ns_ref[i] = ctx(i + t_len)


def _conv_sample(state_t, a_t, g_t, w, b, lg, lb, layer):
    _, n_hist, bd, c = state_t.shape
    t_len = a_t.shape[0]
    assert n_hist == CONV_WIDTH - 1
    tok = pl.BlockSpec((t_len, bd, c), lambda i: (0, 0, 0))
    vec = pl.BlockSpec((1, c), lambda i: (0, 0))
    return pl.pallas_call(
        _conv_sample_kernel,
        grid=(1,),
        in_specs=[pl.BlockSpec((None, n_hist, bd, c), lambda i: (layer, 0, 0, 0)), tok, tok,
                  pl.BlockSpec((CONV_WIDTH, c), lambda i: (0, 0)), vec, vec, vec],
        out_specs=[tok, pl.BlockSpec((n_hist, bd, c), lambda i: (0, 0, 0))],
        out_shape=[jax.ShapeDtypeStruct((t_len, bd, c), F32),
                   jax.ShapeDtypeStruct((n_hist, bd, c), F32)],
        compiler_params=_cparams(1),
        name="conv_sample",
    )(state_t, a_t, g_t, w, b.reshape(1, c), lg.reshape(1, c), lb.reshape(1, c))


def kernel(x_prompt, x_sample, mem_prompt, cache_moba_k, cache_moba_v, page_table, state_swa_w128, state_swa_w512, state_swa_w2048, state_conv, cache_mem_k, cache_mem_v, g_mix, g_mem, w_mem_kv, w_in_a, w_out_a, w_in_b, w_out_b, w_in_c, conv_w, conv_b, conv_ln_g, conv_ln_b, w_out_c, g_ffn, w_ffn_up, w_ffn_down, g_final):
    b, s, d = x_prompt.shape
    bd, t_len, _ = x_sample.shape
    depth = g_mix.shape[0]
    kinds = tuple(i % N_MIXERS for i in range(depth))
    past_len = page_table.shape[1] * PAGE_SIZE
    n_mem = mem_prompt.shape[1]
    bm = 512
    bf = lambda w: w.astype(BF16)

    xp = x_prompt.reshape(b * s, d)
    xs = x_sample.reshape(bd * t_len, d)

    mem_kt, mem_vt = _mem_kv(mem_prompt, g_mem, bf(w_mem_kv))
    cmem_kt = cache_mem_k.transpose(0, 1, 3, 4, 2).reshape(depth, bd, MEM_WIDTH, n_mem)
    cmem_vt = cache_mem_v.transpose(0, 1, 3, 4, 2).reshape(depth, bd, MEM_WIDTH, n_mem)
    n_pool = cache_moba_k.shape[1]
    ckt = cache_moba_k.transpose(0, 1, 3, 4, 2).reshape(-1, n_pool, MIX_WIDTH, PAGE_SIZE)
    cvt = cache_moba_v.transpose(0, 1, 3, 4, 2).reshape(-1, n_pool, MIX_WIDTH, PAGE_SIZE)
    swa_states = (state_swa_w128, state_swa_w512, state_swa_w2048)
    swa_t = [st.transpose(0, 1, 3, 4, 5, 2).reshape(st.shape[0], bd, 2, SWA_WIDTH, st.shape[2])
             for st in swa_states]
    conv_t = state_conv.transpose(0, 2, 1, 3)

    moba_ks, moba_vs = [], []
    kv_t = {0: None, 1: None}
    swa_s = [[] for _ in SWA_GROUPS]
    conv_p, conv_s = [], []

    for l in range(depth):
        kind = kinds[l]
        j = kinds[:l].count(kind)
        if kind == 2:
            w_in = bf(w_in_c[j])
            splits = (MIX_WIDTH, MIX_WIDTH, MEM_WIDTH)
            ap, gp, mqp = _in_proj(xp, g_mix[l], w_in, splits, bm)
            a_s, g_s, mqs = _in_proj(xs, g_mix[l], w_in, splits, bd * t_len)
            mix_p, tail = _conv_prompt(ap.reshape(b, s, -1), gp.reshape(b, s, -1), conv_w[j],
                                       conv_b[j], conv_ln_g[j], conv_ln_b[j], 256)
            mix_p = mix_p.reshape(b * s, -1)
            tb = lambda z: z.reshape(bd, t_len, -1).transpose(1, 0, 2)
            mix_s, new_state = _conv_sample(conv_t, tb(a_s), tb(g_s), conv_w[j], conv_b[j],
                                            conv_ln_g[j], conv_ln_b[j], j)
            mix_s = mix_s.transpose(1, 0, 2).reshape(bd * t_len, -1)
            conv_p.append(tail[:, CONV_HALO - (CONV_WIDTH - 1):])
            conv_s.append(new_state.transpose(1, 0, 2))
            w_out = bf(w_out_c[j])
        else:
            w_in = bf(w_in_a[j] if kind == 0 else w_in_b[j])
            splits = (MIX_WIDTH, MIX_WIDTH, MIX_WIDTH, MEM_WIDTH)
            qp, kp, mqp, kt_all, vt_all = _in_proj_kv(xp, g_mix[l], w_in, kv_t[kind], j,
                                                     kinds.count(kind), b, bm)
            kv_t[kind] = (kt_all, vt_all)
            qs, ks, vs, mqs = _in_proj(xs, g_mix[l], w_in, splits, bd * t_len)
            qp3, kp3 = qp.reshape(b, s, MIX_WIDTH), kp.reshape(b, s, MIX_WIDTH)
            qs3, ks3, vs3 = (z.reshape(bd, t_len, MIX_WIDTH) for z in (qs, ks, vs))
            if kind == 0:
                mix_p = _moba_prompt(qp3, kp3, vt_all, j)
                mix_s = _moba_sample(qs3, ks3, vs3, ckt, cvt, j, page_table)
                moba_ks.append(ks.reshape(bd, t_len, N_MIX_HEADS, HEAD_DIM))
                moba_vs.append(vs.reshape(bd, t_len, N_MIX_HEADS, HEAD_DIM))
                w_out = bf(w_out_a[j])
            else:
                mix_p = _dil_prompt(qp3, kp3, vt_all, j, 256)
                mix_s = _dil_sample(qs3, ks3, vs3, swa_t, j)
                for g, (window, _) in enumerate(SWA_GROUPS):
                    sl = slice(g * SWA_WIDTH, (g + 1) * SWA_WIDTH)
                    heads = lambda z: z.reshape(z.shape[0], z.shape[1], SWA_HEADS, HEAD_DIM)
                    kv_s = jnp.stack([heads(ks3[:, :, sl]), heads(vs3[:, :, sl])], axis=2)
                    assert swa_states[g].shape[2] == window
                    full = jnp.concatenate([swa_states[g][j], kv_s], axis=1)
                    swa_s[g].append(full[:, full.shape[1] - min(window, past_len + t_len):])
                w_out = bf(w_out_b[j])
            mix_p = mix_p.reshape(b * s, -1)
            mix_s = mix_s.reshape(bd * t_len, -1)
        mem_p = _mem_attn(mqp.reshape(b, s, MEM_WIDTH), mem_kt, mem_vt, l, 512)
        mem_s = _mem_attn(mqs.reshape(bd, t_len, MEM_WIDTH), cmem_kt, cmem_vt, l, t_len,
                          bb=8 if bd % 8 == 0 else 1)
        final = l == depth - 1
        wup, wdn = bf(w_ffn_up[l]), bf(w_ffn_down[l])
        xp = _out_ffn(xp, mix_p, mem_p.reshape(b * s, MEM_WIDTH), w_out, g_ffn[l], wup, wdn,
                      g_final, final, bm)
        xs = _out_ffn(xs, mix_s, mem_s.reshape(bd * t_len, MEM_WIDTH), w_out, g_ffn[l], wup, wdn,
                      g_final, final, bd * t_len)

    kt_a, vt_a = kv_t[0]
    token_major = lambda zt: zt.reshape(zt.shape[0], b, -1, HEAD_DIM, s).transpose(0, 1, 4, 2, 3)
    moba_kp, moba_vp = token_major(kt_a), token_major(vt_a)
    kt_b, vt_b = kv_t[1]
    swa_p = []
    for g, (window, _) in enumerate(SWA_GROUPS):
        sl = slice(g * SWA_WIDTH, (g + 1) * SWA_WIDTH)
        last = s - min(window, s)
        kv = jnp.stack([kt_b[:, :, sl, last:], vt_b[:, :, sl, last:]], axis=2)
        kv = kv.reshape(kv.shape[0], b, 2, SWA_HEADS, HEAD_DIM, s - last)
        swa_p.append(kv.transpose(0, 1, 5, 2, 3, 4))

    heads_t = lambda zt: zt.reshape(depth, b, N_MEM_HEADS, HEAD_DIM, n_mem).transpose(0, 1, 4, 2, 3)
    return (xp.reshape(b, s, d), xs.reshape(bd, t_len, d),
            moba_kp, moba_vp, jnp.stack(moba_ks), jnp.stack(moba_vs),
            swa_p[0], swa_p[1], swa_p[2],
            jnp.stack(swa_s[0]), jnp.stack(swa_s[1]), jnp.stack(swa_s[2]),
            jnp.stack(conv_p), jnp.stack(conv_s),
            heads_t(mem_kt), heads_t(mem_vt))
```

```python
import functools

import jax
import jax.numpy as jnp
from jax import lax
from jax.experimental import pallas as pl
from jax.experimental.pallas import tpu as pltpu

F32 = jnp.float32
BF16 = jnp.bfloat16

HEAD_DIM = 64
N_MIX_HEADS = 12
MIX_WIDTH = N_MIX_HEADS * HEAD_DIM
N_MEM_HEADS = 4
MEM_WIDTH = N_MEM_HEADS * HEAD_DIM
MOBA_BLOCK = 256
MOBA_TOPK = 3
PAGE_SIZE = 128
SWA_GROUPS = ((128, 1), (512, 4), (2048, 16))
SWA_HEADS = 4
SWA_WIDTH = SWA_HEADS * HEAD_DIM
CONV_WIDTH = 31
CONV_HALO = 32
SUBLANES = 8
N_MIXERS = 3
ATTN_SCALE = HEAD_DIM ** -0.5
NEG = -1e30
RMS_EPS = 1e-6
LN_EPS = 1e-5

LANES = 128
VMEM_LIMIT = 48 * 1024 * 1024
NT_DIMS = (((1,), (1,)), ((), ()))
HIGHEST = lax.Precision.HIGHEST


def _cparams(n_grid):
    return pltpu.CompilerParams(dimension_semantics=("arbitrary",) * n_grid,
                                vmem_limit_bytes=VMEM_LIMIT)


def _rms(x, g):
    return x * lax.rsqrt(jnp.mean(x * x, axis=-1, keepdims=True) + RMS_EPS) * g


def _dot(a, b, precision=None):
    return jnp.dot(a, b, precision=precision, preferred_element_type=F32)


def _dot_nt(a, b, precision=None):
    return lax.dot_general(a, b, NT_DIMS, precision=precision, preferred_element_type=F32)


def _sigmoid(x):
    return 1.0 / (1.0 + jnp.exp(-x))


def _iota(shape, axis):
    return lax.broadcasted_iota(jnp.int32, shape, axis)


def _in_proj_kernel(x_ref, g_ref, w_ref, *out_refs, splits):
    h = _rms(x_ref[...], g_ref[...]).astype(BF16)
    off = 0
    for o_ref, n in zip(out_refs, splits):
        o_ref[...] = _dot(h, w_ref[:, off:off + n])
        off += n


def _in_proj(x, g, w_bf16, splits, bm):
    m, d = x.shape
    n = w_bf16.shape[1]
    assert sum(splits) == n and m % bm == 0
    return pl.pallas_call(
        functools.partial(_in_proj_kernel, splits=splits),
        grid=(m // bm,),
        in_specs=[pl.BlockSpec((bm, d), lambda i: (i, 0)),
                  pl.BlockSpec((1, d), lambda i: (0, 0)),
                  pl.BlockSpec((d, n), lambda i: (0, 0), pipeline_mode=pl.Buffered(1))],
        out_specs=[pl.BlockSpec((bm, s), lambda i: (i, 0)) for s in splits],
        out_shape=[jax.ShapeDtypeStruct((m, s), F32) for s in splits],
        compiler_params=_cparams(1),
        name="in_proj",
    )(x, g.reshape(1, d), w_bf16)


def _in_proj_kv_kernel(x_ref, g_ref, w_ref, *refs):
    q_ref, k_ref, mq_ref, kt_ref, vt_ref = refs[-5:]
    h = _rms(x_ref[...], g_ref[...]).astype(BF16)
    q_ref[...] = _dot(h, w_ref[:, 0:MIX_WIDTH])
    k = _dot(h, w_ref[:, MIX_WIDTH:2 * MIX_WIDTH])
    k_ref[...] = k
    kt_ref[0, 0] = k.T
    vt_ref[0, 0] = _dot(h, w_ref[:, 2 * MIX_WIDTH:3 * MIX_WIDTH]).T
    mq_ref[...] = _dot(h, w_ref[:, 3 * MIX_WIDTH:])


def _in_proj_kv(x, g, w_bf16, kv_t, layer, n_layers, batch, bm):
    m, d = x.shape
    n = w_bf16.shape[1]
    s = m // batch
    tiles = s // bm
    assert n == 3 * MIX_WIDTH + MEM_WIDTH and m == batch * s and s % bm == 0
    rows = lambda w: pl.BlockSpec((bm, w), lambda i: (i, 0))
    t_spec = pl.BlockSpec((1, 1, MIX_WIDTH, bm), lambda i: (layer, i // tiles, 0, i % tiles))
    t_shape = jax.ShapeDtypeStruct((n_layers, batch, MIX_WIDTH, s), F32)
    in_specs = [rows(d), pl.BlockSpec((1, d), lambda i: (0, 0)),
                pl.BlockSpec((d, n), lambda i: (0, 0), pipeline_mode=pl.Buffered(1))]
    args = [x, g.reshape(1, d), w_bf16]
    aliases = {}
    if kv_t is not None:
        in_specs += [pl.BlockSpec(memory_space=pl.ANY)] * 2
        args += list(kv_t)
        aliases = {3: 3, 4: 4}
    return pl.pallas_call(
        _in_proj_kv_kernel,
        grid=(m // bm,),
        in_specs=in_specs,
        out_specs=[rows(MIX_WIDTH), rows(MIX_WIDTH), rows(MEM_WIDTH), t_spec, t_spec],
        out_shape=[jax.ShapeDtypeStruct((m, MIX_WIDTH), F32), jax.ShapeDtypeStruct((m, MIX_WIDTH), F32),
                   jax.ShapeDtypeStruct((m, MEM_WIDTH), F32), t_shape, t_shape],
        input_output_aliases=aliases,
        compiler_params=_cparams(1),
        name="in_proj_kv",
    )(*args)


def _out_ffn_kernel(x_ref, mix_ref, mem_ref, wo_mix_ref, wo_mem_ref, g_ref, wup_ref, wdn_ref,
                    gfin_ref, o_ref, *, ff_chunk, final):
    x = (x_ref[...] + _dot(mix_ref[...].astype(BF16), wo_mix_ref[...])
         + _dot(mem_ref[...].astype(BF16), wo_mem_ref[...]))
    h = _rms(x, g_ref[...]).astype(BF16)
    o_ref[...] = x
    d_ff = wup_ref.shape[1]
    for c in range(d_ff // ff_chunk):
        u = _dot(h, wup_ref[:, c * ff_chunk:(c + 1) * ff_chunk])
        a = jnp.square(jnp.maximum(u, 0.0)).astype(BF16)
        o_ref[...] += _dot(a, wdn_ref[c * ff_chunk:(c + 1) * ff_chunk, :])
    if final:
        o_ref[...] = _rms(o_ref[...], gfin_ref[...])


def _out_ffn(x, mix, mem, wo_bf16, g, wup_bf16, wdn_bf16, g_final, final, bm):
    m, d = x.shape
    wmix = mix.shape[1]
    wmem = mem.shape[1]
    d_ff = wup_bf16.shape[1]
    const = lambda i: (0, 0)
    single = pl.Buffered(1)
    return pl.pallas_call(
        functools.partial(_out_ffn_kernel, ff_chunk=1024, final=final),
        grid=(m // bm,),
        in_specs=[pl.BlockSpec((bm, d), lambda i: (i, 0)),
                  pl.BlockSpec((bm, wmix), lambda i: (i, 0)),
                  pl.BlockSpec((bm, wmem), lambda i: (i, 0)),
                  pl.BlockSpec((wmix, d), const, pipeline_mode=single),
                  pl.BlockSpec((wmem, d), const, pipeline_mode=single),
                  pl.BlockSpec((1, d), const),
                  pl.BlockSpec((d, d_ff), const, pipeline_mode=single),
                  pl.BlockSpec((d_ff, d), const, pipeline_mode=single),
                  pl.BlockSpec((1, d), const)],
        out_specs=pl.BlockSpec((bm, d), lambda i: (i, 0)),
        out_shape=jax.ShapeDtypeStruct((m, d), F32),
        compiler_params=_cparams(1),
        name="out_ffn",
    )(x, mix, mem, wo_bf16[:wmix], wo_bf16[wmix:], g.reshape(1, d), wup_bf16, wdn_bf16,
      g_final.reshape(1, d))


def _mem_kv_kernel(mem_ref, g_ref, w_ref, kt_ref, vt_ref):
    h = _rms(mem_ref[0], g_ref[0]).astype(BF16)
    zt = _dot(h, w_ref[0]).T
    kt_ref[0, 0] = zt[:MEM_WIDTH]
    vt_ref[0, 0] = zt[MEM_WIDTH:]


def _mem_kv(mem, g_mem, w_bf16):
    b, n_mem, d = mem.shape
    depth = w_bf16.shape[0]
    out = jax.ShapeDtypeStruct((depth, b, MEM_WIDTH, n_mem), F32)
    return pl.pallas_call(
        _mem_kv_kernel,
        grid=(depth, b),
        in_specs=[pl.BlockSpec((1, n_mem, d), lambda l, i: (i, 0, 0)),
                  pl.BlockSpec((1, 1, d), lambda l, i: (l, 0, 0)),
                  pl.BlockSpec((1, d, 2 * MEM_WIDTH), lambda l, i: (l, 0, 0))],
        out_specs=[pl.BlockSpec((1, 1, MEM_WIDTH, n_mem), lambda l, i: (l, i, 0, 0))] * 2,
        out_shape=[out, out],
        compiler_params=_cparams(2),
        name="mem_kv",
    )(mem, g_mem.reshape(depth, 1, d), w_bf16)


def _mem_attn_kernel(q_ref, kt_ref, vt_ref, o_ref):
    for i in range(q_ref.shape[0]):
        q = q_ref[i]
        kt = kt_ref[i].astype(BF16)
        vt = vt_ref[i].astype(BF16)
        lane_head = _iota(q.shape, 1) // HEAD_DIM
        out = jnp.zeros(q.shape, F32)
        for h in range(N_MEM_HEADS):
            qh = jnp.where(lane_head == h, q * ATTN_SCALE, 0.0).astype(BF16)
            s = _dot(qh, kt)
            p = jnp.exp(s - jnp.max(s, axis=-1, keepdims=True))
            l = jnp.sum(p, axis=-1, keepdims=True)
            o = _dot_nt(p.astype(BF16), vt)
            out = jnp.where(lane_head == h, o / l, out)
        o_ref[i] = out


def _mem_attn(q3, kt_all, vt_all, layer, tq, bb=1):
    nb, s, w = q3.shape
    n_mem = kt_all.shape[-1]
    assert nb % bb == 0 and s % tq == 0
    kv_spec = pl.BlockSpec((None, bb, w, n_mem), lambda b, i: (layer, b, 0, 0))
    return pl.pallas_call(
        _mem_attn_kernel,
        grid=(nb // bb, s // tq),
        in_specs=[pl.BlockSpec((bb, tq, w), lambda b, i: (b, i, 0)), kv_spec, kv_spec],
        out_specs=pl.BlockSpec((bb, tq, w), lambda b, i: (b, i, 0)),
        out_shape=jax.ShapeDtypeStruct((nb, s, w), F32),
        compiler_params=_cparams(2),
        name="mem_attn",
    )(q3, kt_all, vt_all)


LOG2E = 1.4426950408889634
Q_SCALE_LOG2 = ATTN_SCALE * LOG2E


def _attend(qb, kb, vt, ok):
    s = jnp.where(ok, _dot_nt(kb, qb), NEG)
    m = jnp.max(s, axis=0, keepdims=True)
    p = jnp.exp2(s - m)
    l = jnp.sum(p, axis=0, keepdims=True)
    acc = _dot(vt, p.astype(BF16))
    return m, l, acc


def _moba_prompt_kernel(q_ref, k_ref, vt_in_ref, o_ref, kmean_ref, kb_ref, vt_ref):
    qi = pl.program_id(2)
    blk = MOBA_BLOCK
    nb = k_ref.shape[1] // blk

    @pl.when(qi == 0)
    def _():
        for n in range(nb):
            rows = slice(n * blk, (n + 1) * blk)
            kblk = k_ref[0, rows, :]
            kmean_ref[n:n + 1, :] = jnp.mean(kblk, axis=0, keepdims=True)
            kb_ref[rows, :] = kblk.astype(BF16)
        vt_ref[...] = vt_in_ref[0].astype(BF16)

    q = q_ref[0]
    lane = _iota((blk, LANES), 1)
    lane_nb = _iota((nb, LANES), 1)
    blk_id = _iota((nb, blk), 0)
    kmean = kmean_ref[...]
    valid = blk_id < qi
    qbs, sels = [], []
    for e in range(2):
        qe = jnp.where(lane // HEAD_DIM == e, q, 0.0)
        kme = jnp.where(lane_nb // HEAD_DIM == e, kmean, 0.0)
        gate = jnp.where(valid, _dot_nt(kme, qe, precision=HIGHEST), NEG)
        cnt = jnp.zeros((nb, blk), F32)
        for m in range(nb):
            gm = gate[m:m + 1, :]
            beats = (gm > gate) | ((gm == gate) & (blk_id > m))
            cnt = cnt + jnp.where(beats, 1.0, 0.0)
        sels.append(jnp.where((cnt < MOBA_TOPK) & valid, 1.0, 0.0))
        qbs.append((qe * Q_SCALE_LOG2).astype(BF16))

    qb2 = jnp.concatenate(qbs, axis=0)
    sel2 = jnp.concatenate(sels, axis=1)
    causal = _iota((blk, blk), 0) <= _iota((blk, blk), 1)
    causal2 = jnp.where(jnp.concatenate([causal, causal], axis=1), 1.0, 0.0)
    feat = _iota((LANES, blk), 0)

    for n_past in range(nb):
        @pl.when(qi == n_past)
        def _(n_past=n_past):
            n_keys = (n_past + 1) * blk
            ok = jnp.concatenate([jnp.broadcast_to(sel2[n:n + 1, :], (blk, 2 * blk))
                                  for n in range(n_past)] + [causal2], axis=0) > 0.5
            _, l, acc = _attend(qb2, kb_ref[0:n_keys, :], vt_ref[:, 0:n_keys], ok)
            out = acc / l
            o_ref[0] = jnp.where(feat < HEAD_DIM, out[:, :blk], out[:, blk:]).T


def _moba_prompt(q3, k3, vt_all, layer):
    b, s, w = q3.shape
    blk = MOBA_BLOCK
    nb = s // blk
    assert s % blk == 0 and w % LANES == 0 and vt_all.shape[1:] == (b, w, s)
    k_spec = pl.BlockSpec((1, s, LANES), lambda i, p, j: (i, 0, p))
    vt_spec = pl.BlockSpec((None, 1, LANES, s), lambda i, p, j: (layer, i, p, 0))
    return pl.pallas_call(
        _moba_prompt_kernel,
        grid=(b, w // LANES, nb),
        in_specs=[pl.BlockSpec((1, blk, LANES), lambda i, p, j: (i, j, p)), k_spec, vt_spec],
        out_specs=pl.BlockSpec((1, blk, LANES), lambda i, p, j: (i, j, p)),
        out_shape=jax.ShapeDtypeStruct((b, s, w), F32),
        scratch_shapes=[pltpu.VMEM((nb, LANES), F32),
                        pltpu.VMEM((s, LANES), BF16),
                        pltpu.VMEM((LANES, s), BF16)],
        compiler_params=_cparams(3),
        name="moba_prompt",
    )(q3, k3, vt_all)


def _block_diag_q(q, n_heads, rows_per_tok):
    t_len, w = q.shape
    hrow = _iota((rows_per_tok, w), 0)
    lane_head = _iota((rows_per_tok, w), 1) // HEAD_DIM
    keep = (hrow == lane_head) & (hrow < n_heads)
    parts = [jnp.where(keep, jnp.broadcast_to(q[t:t + 1, :], (rows_per_tok, w)), 0.0)
             for t in range(t_len)]
    return jnp.concatenate(parts, axis=0)


def _head_diag_rows(o, t_len, rows_per_tok):
    w = o.shape[1]
    hrow = _iota((rows_per_tok, w), 0)
    lane_head = _iota((rows_per_tok, w), 1) // HEAD_DIM
    keep = hrow == lane_head
    return [jnp.sum(jnp.where(keep, o[t * rows_per_tok:(t + 1) * rows_per_tok, :], 0.0), axis=0,
                    keepdims=True) for t in range(t_len)]


MOBA_QROWS = 16
MOBA_CHUNK = 8
MOBA_AHEAD = 3
MOBA_SLOTS = MOBA_AHEAD + 1


def _moba_sample_kernel(pt_ref, q_ref, kn_ref, vn_ref, kt_hbm, vt_hbm, o_ref,
                        buf_ref, sem, sc_ref, ksum_ref, acc_ref, *, layer, n_blocks):
    b = pl.program_id(0)
    n_batch = pl.num_programs(0)
    t_len = q_ref.shape[1]
    nr = t_len * MOBA_QROWS
    chunk = MOBA_CHUNK
    k_chunks = 2 * n_blocks // chunk
    per_b = 2 * k_chunks
    total = n_batch * per_b

    def page_copy(src_hbm, bb, page, slot, j):
        return pltpu.make_async_copy(src_hbm.at[layer, pt_ref[bb, page]], buf_ref.at[slot, j],
                                     sem.at[slot])

    def start_chunk(c):
        bb = c // per_b
        i = c % per_b
        slot = c % MOBA_SLOTS

        @pl.when(i < k_chunks)
        def _():
            for j in range(chunk):
                page_copy(kt_hbm, bb, i * chunk + j, slot, j).start()

        @pl.when(i >= k_chunks)
        def _():
            for j in range(chunk):
                page_copy(vt_hbm, bb, (i - k_chunks) * chunk + j, slot, j).start()

    def next_chunk(c):
        @pl.when(c + MOBA_AHEAD < total)
        def _():
            start_chunk(c + MOBA_AHEAD)

        slot = c % MOBA_SLOTS
        for j in range(chunk):
            page_copy(kt_hbm, 0, 0, slot, j).wait()
        return slot

    @pl.when(b == 0)
    def _():
        for c in range(MOBA_AHEAD):
            start_chunk(c)

    qbd = _block_diag_q(q_ref[0], N_MIX_HEADS, MOBA_QROWS)
    qb = (qbd * ATTN_SCALE).astype(BF16)
    lane_w = _iota(ksum_ref.shape, 1)

    def k_body(i, carry):
        slot = next_chunk(b * per_b + i)
        for jj in range(chunk // 2):
            ka = buf_ref[slot, 2 * jj]
            kb = buf_ref[slot, 2 * jj + 1]
            page = i * chunk + 2 * jj
            sc_ref[page] = _dot(qb, ka.astype(BF16))
            sc_ref[page + 1] = _dot(qb, kb.astype(BF16))
            col = jnp.sum(ka + kb, axis=1, keepdims=True)
            ksum_ref[...] = jnp.where(lane_w == page // 2, col, ksum_ref[...])
        return carry

    ksum_ref[...] = jnp.zeros(ksum_ref.shape, F32)
    lax.fori_loop(0, k_chunks, k_body, 0)

    lane = _iota((nr, LANES), 1)
    gmat = _dot(qbd, ksum_ref[...], precision=HIGHEST) * (1.0 / MOBA_BLOCK)
    gmat = jnp.where(lane < n_blocks, gmat, NEG)
    cnt = jnp.zeros((nr, LANES), F32)
    for m in range(n_blocks):
        gm = jnp.broadcast_to(gmat[:, m:m + 1], (nr, LANES))
        beats = (gm > gmat) | ((gm == gmat) & (lane > m))
        cnt = cnt + jnp.where(beats, 1.0, 0.0)
    sel = jnp.where((cnt < MOBA_TOPK) & (lane < n_blocks), 1.0, 0.0)

    def block_ok(n):
        return jnp.broadcast_to(sel[:, n:n + 1], (nr, LANES)) > 0.5

    trow = _iota((nr, 1), 0) // MOBA_QROWS
    s_own = []
    for u in range(t_len):
        su = jnp.sum(qbd * kn_ref[0, u:u + 1, :], axis=1, keepdims=True) * ATTN_SCALE
        s_own.append(jnp.where(trow >= u, su, NEG))

    mx = jnp.full((nr, LANES), NEG, F32)
    for n in range(n_blocks):
        ok = block_ok(n)
        for half in range(2):
            mx = jnp.maximum(mx, jnp.where(ok, sc_ref[2 * n + half], NEG))
    m = jnp.max(mx, axis=1, keepdims=True)
    for su in s_own:
        m = jnp.maximum(m, su)
    lsum = jnp.zeros((nr, LANES), F32)
    for n in range(n_blocks):
        ok = block_ok(n)
        for half in range(2):
            p = jnp.exp(jnp.where(ok, sc_ref[2 * n + half], NEG) - m)
            sc_ref[2 * n + half] = p
            lsum = lsum + p
    l = jnp.sum(lsum, axis=1, keepdims=True)
    acc = jnp.zeros(acc_ref.shape, F32)
    for u in range(t_len):
        pu = jnp.exp(s_own[u] - m)
        l = l + pu
        acc = acc + pu * vn_ref[0, u:u + 1, :]
    acc_ref[...] = acc

    def v_body(i, carry):
        slot = next_chunk(b * per_b + k_chunks + i)
        for j in range(chunk):
            p = sc_ref[i * chunk + j].astype(BF16)
            acc_ref[...] += _dot_nt(p, buf_ref[slot, j].astype(BF16))
        return carry

    lax.fori_loop(0, k_chunks, v_body, 0)

    o = acc_ref[...] * (1.0 / l)
    for t, r in enumerate(_head_diag_rows(o, t_len, MOBA_QROWS)):
        o_ref[0, t:t + 1, :] = r


def _moba_sample(q3, kn3, vn3, ckt, cvt, layer, page_table):
    bd, t_len, w = q3.shape
    n_pages = page_table.shape[1]
    pages_per_block = MOBA_BLOCK // PAGE_SIZE
    assert pages_per_block == 2 and n_pages % pages_per_block == 0 and t_len <= MOBA_BLOCK
    nb = n_pages // pages_per_block
    assert n_pages % MOBA_CHUNK == 0 and MOBA_CHUNK % pages_per_block == 0 and nb <= LANES
    nr = t_len * MOBA_QROWS
    tok = pl.BlockSpec((1, t_len, w), lambda b, pt: (b, 0, 0))
    hbm = pl.BlockSpec(memory_space=pl.ANY)
    grid_spec = pltpu.PrefetchScalarGridSpec(
        num_scalar_prefetch=1,
        grid=(bd,),
        in_specs=[tok, tok, tok, hbm, hbm],
        out_specs=tok,
        scratch_shapes=[pltpu.VMEM((MOBA_SLOTS, MOBA_CHUNK, w, PAGE_SIZE), F32),
                        pltpu.SemaphoreType.DMA((MOBA_SLOTS,)),
                        pltpu.VMEM((n_pages, nr, PAGE_SIZE), F32),
                        pltpu.VMEM((w, LANES), F32),
                        pltpu.VMEM((nr, w), F32)])
    return pl.pallas_call(
        functools.partial(_moba_sample_kernel, layer=layer, n_blocks=nb),
        grid_spec=grid_spec,
        out_shape=jax.ShapeDtypeStruct((bd, t_len, w), F32),
        compiler_params=_cparams(1),
        name="moba_sample",
    )(page_table, q3, kn3, vn3, ckt, cvt)


def _dil_prompt_kernel(q0_ref, q1_ref, q2_ref, k0_ref, k1_ref, k2_ref, v0_ref, v1_ref, v2_ref,
                       o_ref, kb_ref, vt_ref, og_ref, lse_ref):
    qi = pl.program_id(2)
    tq = q0_ref.shape[1]
    s_len = k0_ref.shape[1]
    q_refs = (q0_ref, q1_ref, q2_ref)

    @pl.when(qi == 0)
    def _():
        for g, (k_ref, v_ref) in enumerate(((k0_ref, v0_ref), (k1_ref, v1_ref), (k2_ref, v2_ref))):
            kb_ref[g] = k_ref[0].astype(BF16)
            for n in range(s_len // LANES):
                vt_ref[g, n] = v_ref[0, :, n * LANES:(n + 1) * LANES].astype(BF16)

    lane = _iota((tq, LANES), 1)
    t0 = qi * tq

    def group(g, k0, n_keys, dist0):
        window, dil = SWA_GROUPS[g]
        d = dist0 + _iota((n_keys, tq), 1) - _iota((n_keys, tq), 0)
        ok = (d >= 0) & (d <= window)
        if dil > 1:
            ok = ok & ((d & (dil - 1)) == 0)
        okf = jnp.where(ok, 1.0, 0.0)
        ok2 = jnp.concatenate([okf, okf], axis=1) > 0.5
        kb = kb_ref[g, pl.ds(k0, n_keys), :]
        vt = jnp.concatenate([vt_ref[g, k0 // LANES + i] for i in range(n_keys // LANES)], axis=1)
        qb2 = jnp.concatenate(
            [jnp.where(lane // HEAD_DIM == e, q_refs[g][0] * Q_SCALE_LOG2, 0.0).astype(BF16)
             for e in range(2)], axis=0)
        m, l, acc = _attend(qb2, kb, vt, ok2)
        return acc / l, m + jnp.log2(l)

    os = [[], []]
    lses = [[], []]
    for g, (window, dil) in enumerate(SWA_GROUPS):
        if window + tq < s_len:
            n_keys = window + tq
            k0 = pl.multiple_of(jnp.maximum(t0 + tq - n_keys, 0), LANES)
            o2, lse2 = group(g, k0, n_keys, t0 - k0)
        else:
            for qv in range(s_len // tq):
                @pl.when(qi == qv)
                def _(qv=qv, g=g, window=window):
                    end = (qv + 1) * tq
                    n_keys = min(window + tq, end)
                    o2, lse2 = group(g, end - n_keys, n_keys, qv * tq - (end - n_keys))
                    og_ref[...] = o2
                    lse_ref[0:1, :] = lse2
            o2 = og_ref[...]
            lse2 = lse_ref[0:1, :]
        for e in range(2):
            os[e].append(o2[:, e * tq:(e + 1) * tq])
            lses[e].append(lse2[:, e * tq:(e + 1) * tq])
    outs = []
    for e in range(2):
        mx = jnp.maximum(jnp.maximum(lses[e][0], lses[e][1]), lses[e][2])
        ws = [jnp.exp2(x - mx) for x in lses[e]]
        tot = ws[0] + ws[1] + ws[2]
        outs.append((ws[0] * os[e][0] + ws[1] * os[e][1] + ws[2] * os[e][2]) / tot)
    feat = _iota((LANES, tq), 0)
    o_ref[0] = jnp.where(feat < HEAD_DIM, outs[0], outs[1]).T


def _dil_prompt(q3, k3, vt_all, layer, tq):
    b, s, w = q3.shape
    n_pairs = SWA_WIDTH // LANES
    assert w == len(SWA_GROUPS) * SWA_WIDTH and s % tq == 0 and vt_all.shape[1:] == (b, w, s)
    assert all(d & (d - 1) == 0 and wd % LANES == 0 for wd, d in SWA_GROUPS) and tq % LANES == 0

    def q_spec(g):
        return pl.BlockSpec((1, tq, LANES), lambda i, p, j: (i, j, g * n_pairs + p))

    def kv_spec(g):
        return pl.BlockSpec((1, s, LANES), lambda i, p, j: (i, 0, g * n_pairs + p))

    def vt_spec(g):
        return pl.BlockSpec((None, 1, LANES, s), lambda i, p, j: (layer, i, g * n_pairs + p, 0))

    n_g = len(SWA_GROUPS)
    return pl.pallas_call(
        _dil_prompt_kernel,
        grid=(b, n_pairs, s // tq),
        in_specs=([q_spec(g) for g in range(n_g)] + [kv_spec(g) for g in range(n_g)]
                  + [vt_spec(g) for g in range(n_g)]),
        out_specs=pl.BlockSpec((1, tq, LANES), lambda i, p, j: (i, j, p)),
        out_shape=jax.ShapeDtypeStruct((b, s, SWA_WIDTH), F32),
        scratch_shapes=[pltpu.VMEM((n_g, s, LANES), BF16),
                        pltpu.VMEM((n_g, s // LANES, LANES, LANES), BF16),
                        pltpu.VMEM((LANES, 2 * tq), F32),
                        pltpu.VMEM((SUBLANES, 2 * tq), F32)],
        compiler_params=_cparams(3),
        name="dil_prompt",
    )(q3, q3, q3, k3, k3, k3, vt_all, vt_all, vt_all)


SWA_QROWS = 8


def _dil_sample_kernel(q_ref, kn_ref, vn_ref, b0_ref, b1_ref, b2_ref, o_ref):
    t_len = q_ref.shape[1]
    nr = t_len * SWA_QROWS
    q = q_ref[0]
    kn = kn_ref[0]
    vn = vn_ref[0]
    trow = _iota((nr, 1), 0) // SWA_QROWS
    os, lses = [], []
    for g, ((window, dil), buf_ref) in enumerate(zip(SWA_GROUPS, (b0_ref, b1_ref, b2_ref))):
        sl = slice(g * SWA_WIDTH, (g + 1) * SWA_WIDTH)
        qbd = _block_diag_q(q[:, sl], SWA_HEADS, SWA_QROWS)
        kt = buf_ref[0, 0].astype(BF16)
        vt = buf_ref[0, 1].astype(BF16)
        s = _dot((qbd * ATTN_SCALE).astype(BF16), kt)
        d = window + (_iota((nr, window), 0) // SWA_QROWS) - _iota((nr, window), 1)
        s = jnp.where((d <= window) & ((d & (dil - 1)) == 0), s, NEG)
        s_new = []
        for u in range(t_len):
            su = jnp.sum(qbd * kn[u:u + 1, sl], axis=1, keepdims=True) * ATTN_SCALE
            du = trow - u
            s_new.append(jnp.where((du >= 0) & ((du & (dil - 1)) == 0), su, NEG))
        m = jnp.max(s, axis=1, keepdims=True)
        for su in s_new:
            m = jnp.maximum(m, su)
        p = jnp.exp(s - m)
        l = jnp.sum(p, axis=1, keepdims=True)
        o = _dot_nt(p.astype(BF16), vt)
        for u in range(t_len):
            pu = jnp.exp(s_new[u] - m)
            l = l + pu
            o = o + pu * vn[u:u + 1, sl]
        os.append(o / l)
        lses.append(m + jnp.log(l))
    mx = jnp.maximum(jnp.maximum(lses[0], lses[1]), lses[2])
    ws = [jnp.exp(x - mx) for x in lses]
    tot = ws[0] + ws[1] + ws[2]
    out = (ws[0] * os[0] + ws[1] * os[1] + ws[2] * os[2]) / tot
    for t, r in enumerate(_head_diag_rows(out, t_len, SWA_QROWS)):
        o_ref[0, t:t + 1, :] = r


def _dil_sample(q3, kn3, vn3, bufs_t, layer):
    bd, t_len, w = q3.shape
    tok = pl.BlockSpec((1, t_len, w), lambda b: (b, 0, 0))
    buf_specs = []
    for (window, _), buf in zip(SWA_GROUPS, bufs_t):
        assert buf.shape[1:] == (bd, 2, SWA_WIDTH, window)
        buf_specs.append(pl.BlockSpec((None, 1, 2, SWA_WIDTH, window),
                                      lambda b: (layer, b, 0, 0, 0)))
    return pl.pallas_call(
        _dil_sample_kernel,
        grid=(bd,),
        in_specs=[tok, tok, tok] + buf_specs,
        out_specs=pl.BlockSpec((1, t_len, SWA_WIDTH), lambda b: (b, 0, 0)),
        out_shape=jax.ShapeDtypeStruct((bd, t_len, SWA_WIDTH), F32),
        compiler_params=_cparams(1),
        name="dil_sample",
    )(q3, kn3, vn3, *bufs_t)


def _ln_swish(y, lg, lb):
    mu = jnp.mean(y, axis=-1, keepdims=True)
    yc = y - mu
    var = jnp.mean(yc * yc, axis=-1, keepdims=True)
    z = yc * lax.rsqrt(var + LN_EPS) * lg + lb
    return z * _sigmoid(z)


def _conv_prompt_kernel(a_ref, g_ref, w_ref, b_ref, lg_ref, lb_ref, o_ref, tail_ref, ctx_ref,
                        sh_ref):
    qi = pl.program_id(1)
    tq = a_ref.shape[1]
    halo = CONV_HALO
    sub = SUBLANES

    @pl.when(qi == 0)
    def _():
        ctx_ref[0:halo, :] = jnp.zeros((halo, ctx_ref.shape[1]), F32)

    ctx_ref[halo:halo + tq, :] = a_ref[0] * _sigmoid(g_ref[0])
    n_sh = sh_ref.shape[1]
    for r in range(1, sub):
        sh_ref[r - 1] = ctx_ref[r:r + n_sh, :]
    first = halo - (CONV_WIDTH - 1)
    acc = jnp.zeros((tq, ctx_ref.shape[1]), F32)
    for j in range(CONV_WIDTH):
        start, phase = (first + j) // sub * sub, (first + j) % sub
        src = ctx_ref[start:start + tq, :] if phase == 0 else sh_ref[phase - 1, start:start + tq, :]
        acc = acc + src * w_ref[j:j + 1, :]
    o_ref[0] = _ln_swish(acc + b_ref[...], lg_ref[...], lb_ref[...])
    tail = ctx_ref[tq:tq + halo, :]
    ctx_ref[0:halo, :] = tail
    tail_ref[0] = tail


def _conv_prompt(a3, g3, w, b, lg, lb, tq):
    bsz, s, c = a3.shape
    assert s % tq == 0 and tq >= CONV_HALO and CONV_HALO % SUBLANES == 0
    tile = pl.BlockSpec((1, tq, c), lambda i, j: (i, j, 0))
    vec = pl.BlockSpec((1, c), lambda i, j: (0, 0))
    return pl.pallas_call(
        _conv_prompt_kernel,
        grid=(bsz, s // tq),
        in_specs=[tile, tile, pl.BlockSpec((CONV_WIDTH, c), lambda i, j: (0, 0)), vec, vec, vec],
        out_specs=[tile, pl.BlockSpec((1, CONV_HALO, c), lambda i, j: (i, 0, 0))],
        out_shape=[jax.ShapeDtypeStruct((bsz, s, c), F32),
                   jax.ShapeDtypeStruct((bsz, CONV_HALO, c), F32)],
        scratch_shapes=[pltpu.VMEM((CONV_HALO + tq, c), F32),
                        pltpu.VMEM((SUBLANES - 1, CONV_HALO - SUBLANES + tq, c), F32)],
        compiler_params=_cparams(2),
        name="conv_prompt",
    )(a3, g3, w, b.reshape(1, c), lg.reshape(1, c), lb.reshape(1, c))


def _conv_sample_kernel(st_ref, a_ref, g_ref, w_ref, b_ref, lg_ref, lb_ref, o_ref, ns_ref):
    n_hist = st_ref.shape[0]
    t_len = a_ref.shape[0]
    u = [a_ref[t] * _sigmoid(g_ref[t]) for t in range(t_len)]

    def ctx(i):
        return st_ref[i] if i < n_hist else u[i - n_hist]

    for t in range(t_len):
        acc = jnp.zeros(u[0].shape, F32)
        for j in range(CONV_WIDTH):
            acc = acc + ctx(t + j) * w_ref[j:j + 1, :]
        o_ref[t] = _ln_swish(acc + b_ref[...], lg_ref[...], lb_ref[...])
    for i in range(n_hist):
        ns_ref[i] = ctx(i + t_len)


def _conv_sample(state_t, a_t, g_t, w, b, lg, lb, layer):
    _, n_hist, bd, c = state_t.shape
    t_len = a_t.shape[0]
    assert n_hist == CONV_WIDTH - 1
    tok = pl.BlockSpec((t_len, bd, c), lambda i: (0, 0, 0))
    vec = pl.BlockSpec((1, c), lambda i: (0, 0))
    return pl.pallas_call(
        _conv_sample_kernel,
        grid=(1,),
        in_specs=[pl.BlockSpec((None, n_hist, bd, c), lambda i: (layer, 0, 0, 0)), tok, tok,
                  pl.BlockSpec((CONV_WIDTH, c), lambda i: (0, 0)), vec, vec, vec],
        out_specs=[tok, pl.BlockSpec((n_hist, bd, c), lambda i: (0, 0, 0))],
        out_shape=[jax.ShapeDtypeStruct((t_len, bd, c), F32),
                   jax.ShapeDtypeStruct((n_hist, bd, c), F32)],
        compiler_params=_cparams(1),
        name="conv_sample",
    )(state_t, a_t, g_t, w, b.reshape(1, c), lg.reshape(1, c), lb.reshape(1, c))


def kernel(x_prompt, x_sample, mem_prompt, cache_moba_k, cache_moba_v, page_table, state_swa_w128, state_swa_w512, state_swa_w2048, state_conv, cache_mem_k, cache_mem_v, g_mix, g_mem, w_mem_kv, w_in_a, w_out_a, w_in_b, w_out_b, w_in_c, conv_w, conv_b, conv_ln_g, conv_ln_b, w_out_c, g_ffn, w_ffn_up, w_ffn_down, g_final):
    b, s, d = x_prompt.shape
    bd, t_len, _ = x_sample.shape
    depth = g_mix.shape[0]
    kinds = tuple(i % N_MIXERS for i in range(depth))
    past_len = page_table.shape[1] * PAGE_SIZE
    n_mem = mem_prompt.shape[1]
    bm = 512
    bf = lambda w: w.astype(BF16)

    xp = x_prompt.reshape(b * s, d)
    xs = x_sample.reshape(bd * t_len, d)

    mem_kt, mem_vt = _mem_kv(mem_prompt, g_mem, bf(w_mem_kv))
    cmem_kt = cache_mem_k.transpose(0, 1, 3, 4, 2).reshape(depth, bd, MEM_WIDTH, n_mem)
    cmem_vt = cache_mem_v.transpose(0, 1, 3, 4, 2).reshape(depth, bd, MEM_WIDTH, n_mem)
    n_pool = cache_moba_k.shape[1]
    ckt = cache_moba_k.transpose(0, 1, 3, 4, 2).reshape(-1, n_pool, MIX_WIDTH, PAGE_SIZE)
    cvt = cache_moba_v.transpose(0, 1, 3, 4, 2).reshape(-1, n_pool, MIX_WIDTH, PAGE_SIZE)
    swa_states = (state_swa_w128, state_swa_w512, state_swa_w2048)
    swa_t = [st.transpose(0, 1, 3, 4, 5, 2).reshape(st.shape[0], bd, 2, SWA_WIDTH, st.shape[2])
             for st in swa_states]
    conv_t = state_conv.transpose(0, 2, 1, 3)

    moba_ks, moba_vs = [], []
    kv_t = {}
    for kind in (0, 1):
        n_kind = kinds.count(kind)
        zeros = jnp.zeros((n_kind, b, MIX_WIDTH, s), F32) if n_kind > 1 else None
        kv_t[kind] = None if zeros is None else (zeros, zeros)
    swa_s = [[] for _ in SWA_GROUPS]
    conv_p, conv_s = [], []

    for l in range(depth):
        kind = kinds[l]
        j = kinds[:l].count(kind)
        if kind == 2:
            w_in = bf(w_in_c[j])
            splits = (MIX_WIDTH, MIX_WIDTH, MEM_WIDTH)
            ap, gp, mqp = _in_proj(xp, g_mix[l], w_in, splits, bm)
            a_s, g_s, mqs = _in_proj(xs, g_mix[l], w_in, splits, bd * t_len)
            mix_p, tail = _conv_prompt(ap.reshape(b, s, -1), gp.reshape(b, s, -1), conv_w[j],
                                       conv_b[j], conv_ln_g[j], conv_ln_b[j], 256)
            mix_p = mix_p.reshape(b * s, -1)
            tb = lambda z: z.reshape(bd, t_len, -1).transpose(1, 0, 2)
            mix_s, new_state = _conv_sample(conv_t, tb(a_s), tb(g_s), conv_w[j], conv_b[j],
                                            conv_ln_g[j], conv_ln_b[j], j)
            mix_s = mix_s.transpose(1, 0, 2).reshape(bd * t_len, -1)
            conv_p.append(tail[:, CONV_HALO - (CONV_WIDTH - 1):])
            conv_s.append(new_state.transpose(1, 0, 2))
            w_out = bf(w_out_c[j])
        else:
            w_in = bf(w_in_a[j] if kind == 0 else w_in_b[j])
            splits = (MIX_WIDTH, MIX_WIDTH, MIX_WIDTH, MEM_WIDTH)
            qp, kp, mqp, kt_all, vt_all = _in_proj_kv(xp, g_mix[l], w_in, kv_t[kind], j,
                                                     kinds.count(kind), b, bm)
            kv_t[kind] = (kt_all, vt_all)
            qs, ks, vs, mqs = _in_proj(xs, g_mix[l], w_in, splits, bd * t_len)
            qp3, kp3 = qp.reshape(b, s, MIX_WIDTH), kp.reshape(b, s, MIX_WIDTH)
            qs3, ks3, vs3 = (z.reshape(bd, t_len, MIX_WIDTH) for z in (qs, ks, vs))
            if kind == 0:
                mix_p = _moba_prompt(qp3, kp3, vt_all, j)
                mix_s = _moba_sample(qs3, ks3, vs3, ckt, cvt, j, page_table)
                moba_ks.append(ks.reshape(bd, t_len, N_MIX_HEADS, HEAD_DIM))
                moba_vs.append(vs.reshape(bd, t_len, N_MIX_HEADS, HEAD_DIM))
                w_out = bf(w_out_a[j])
            else:
                mix_p = _dil_prompt(qp3, kp3, vt_all, j, 256)
                mix_s = _dil_sample(qs3, ks3, vs3, swa_t, j)
                for g, (window, _) in enumerate(SWA_GROUPS):
                    sl = slice(g * SWA_WIDTH, (g + 1) * SWA_WIDTH)
                    heads = lambda z: z.reshape(z.shape[0], z.shape[1], SWA_HEADS, HEAD_DIM)
                    kv_s = jnp.stack([heads(ks3[:, :, sl]), heads(vs3[:, :, sl])], axis=2)
                    assert swa_states[g].shape[2] == window
                    full = jnp.concatenate([swa_states[g][j], kv_s], axis=1)
                    swa_s[g].append(full[:, full.shape[1] - min(window, past_len + t_len):])
                w_out = bf(w_out_b[j])
            mix_p = mix_p.reshape(b * s, -1)
            mix_s = mix_s.reshape(bd * t_len, -1)
        mem_p = _mem_attn(mqp.reshape(b, s, MEM_WIDTH), mem_kt, mem_vt, l, 512)
        mem_s = _mem_attn(mqs.reshape(bd, t_len, MEM_WIDTH), cmem_kt, cmem_vt, l, t_len,
                          bb=8 if bd % 8 == 0 else 1)
        final = l == depth - 1
        wup, wdn = bf(w_ffn_up[l]), bf(w_ffn_down[l])
        xp = _out_ffn(xp, mix_p, mem_p.reshape(b * s, MEM_WIDTH), w_out, g_ffn[l], wup, wdn,
                      g_final, final, bm)
        xs = _out_ffn(xs, mix_s, mem_s.reshape(bd * t_len, MEM_WIDTH), w_out, g_ffn[l], wup, wdn,
                      g_final, final, bd * t_len)

    kt_a, vt_a = kv_t[0]
    token_major = lambda zt: zt.reshape(zt.shape[0], b, -1, HEAD_DIM, s).transpose(0, 1, 4, 2, 3)
    moba_kp, moba_vp = token_major(kt_a), token_major(vt_a)
    kt_b, vt_b = kv_t[1]
    swa_p = []
    for g, (window, _) in enumerate(SWA_GROUPS):
        sl = slice(g * SWA_WIDTH, (g + 1) * SWA_WIDTH)
        last = s - min(window, s)
        kv = jnp.stack([kt_b[:, :, sl, last:], vt_b[:, :, sl, last:]], axis=2)
        kv = kv.reshape(kv.shape[0], b, 2, SWA_HEADS, HEAD_DIM, s - last)
        swa_p.append(kv.transpose(0, 1, 5, 2, 3, 4))

    heads_t = lambda zt: zt.reshape(depth, b, N_MEM_HEADS, HEAD_DIM, n_mem).transpose(0, 1, 4, 2, 3)
    return (xp.reshape(b, s, d), xs.reshape(bd, t_len, d),
            moba_kp, moba_vp, jnp.stack(moba_ks), jnp.stack(moba_vs),
            swa_p[0], swa_p[1], swa_p[2],
            jnp.stack(swa_s[0]), jnp.stack(swa_s[1]), jnp.stack(swa_s[2]),
            jnp.stack(conv_p), jnp.stack(conv_s),
            heads_t(mem_kt), heads_t(mem_vt))
```

```python
import functools

import jax
import jax.numpy as jnp
from jax import lax
from jax.experimental import pallas as pl
from jax.experimental.pallas import tpu as pltpu

F32 = jnp.float32
BF16 = jnp.bfloat16

HEAD_DIM = 64
N_MIX_HEADS = 12
MIX_WIDTH = N_MIX_HEADS * HEAD_DIM
N_MEM_HEADS = 4
MEM_WIDTH = N_MEM_HEADS * HEAD_DIM
MOBA_BLOCK = 256
MOBA_TOPK = 3
PAGE_SIZE = 128
SWA_GROUPS = ((128, 1), (512, 4), (2048, 16))
SWA_HEADS = 4
SWA_WIDTH = SWA_HEADS * HEAD_DIM
CONV_WIDTH = 31
CONV_HALO = 32
SUBLANES = 8
N_MIXERS = 3
ATTN_SCALE = HEAD_DIM ** -0.5
NEG = -1e30
RMS_EPS = 1e-6
LN_EPS = 1e-5

LANES = 128
VMEM_LIMIT = 48 * 1024 * 1024
NT_DIMS = (((1,), (1,)), ((), ()))
HIGHEST = lax.Precision.HIGHEST


def _cparams(n_grid):
    return pltpu.CompilerParams(dimension_semantics=("arbitrary",) * n_grid,
                                vmem_limit_bytes=VMEM_LIMIT)


def _rms(x, g):
    return x * lax.rsqrt(jnp.mean(x * x, axis=-1, keepdims=True) + RMS_EPS) * g


def _dot(a, b, precision=None):
    return jnp.dot(a, b, precision=precision, preferred_element_type=F32)


def _dot_nt(a, b, precision=None):
    return lax.dot_general(a, b, NT_DIMS, precision=precision, preferred_element_type=F32)


def _sigmoid(x):
    return 1.0 / (1.0 + jnp.exp(-x))


def _iota(shape, axis):
    return lax.broadcasted_iota(jnp.int32, shape, axis)


def _in_proj_kernel(x_ref, g_ref, w_ref, *out_refs, splits):
    h = _rms(x_ref[...], g_ref[...]).astype(BF16)
    off = 0
    for o_ref, n in zip(out_refs, splits):
        o_ref[...] = _dot(h, w_ref[:, off:off + n])
        off += n


def _in_proj(x, g, w_bf16, splits, bm):
    m, d = x.shape
    n = w_bf16.shape[1]
    assert sum(splits) == n and m % bm == 0
    return pl.pallas_call(
        functools.partial(_in_proj_kernel, splits=splits),
        grid=(m // bm,),
        in_specs=[pl.BlockSpec((bm, d), lambda i: (i, 0)),
                  pl.BlockSpec((1, d), lambda i: (0, 0)),
                  pl.BlockSpec((d, n), lambda i: (0, 0), pipeline_mode=pl.Buffered(1))],
        out_specs=[pl.BlockSpec((bm, s), lambda i: (i, 0)) for s in splits],
        out_shape=[jax.ShapeDtypeStruct((m, s), F32) for s in splits],
        compiler_params=_cparams(1),
        name="in_proj",
    )(x, g.reshape(1, d), w_bf16)


def _in_proj_kv_kernel(x_ref, g_ref, w_ref, *refs):
    q_ref, k_ref, mq_ref, kt_ref, vt_ref = refs[-5:]
    h = _rms(x_ref[...], g_ref[...]).astype(BF16)
    q_ref[...] = _dot(h, w_ref[:, 0:MIX_WIDTH])
    k = _dot(h, w_ref[:, MIX_WIDTH:2 * MIX_WIDTH])
    k_ref[...] = k
    kt_ref[0, 0] = k.T
    vt_ref[0, 0] = _dot(h, w_ref[:, 2 * MIX_WIDTH:3 * MIX_WIDTH]).T
    mq_ref[...] = _dot(h, w_ref[:, 3 * MIX_WIDTH:])


def _in_proj_kv(x, g, w_bf16, kv_t, layer, n_layers, batch, bm):
    m, d = x.shape
    n = w_bf16.shape[1]
    s = m // batch
    tiles = s // bm
    assert n == 3 * MIX_WIDTH + MEM_WIDTH and m == batch * s and s % bm == 0
    rows = lambda w: pl.BlockSpec((bm, w), lambda i: (i, 0))
    t_spec = pl.BlockSpec((1, 1, MIX_WIDTH, bm), lambda i: (layer, i // tiles, 0, i % tiles))
    t_shape = jax.ShapeDtypeStruct((n_layers, batch, MIX_WIDTH, s), F32)
    in_specs = [rows(d), pl.BlockSpec((1, d), lambda i: (0, 0)),
                pl.BlockSpec((d, n), lambda i: (0, 0), pipeline_mode=pl.Buffered(1))]
    args = [x, g.reshape(1, d), w_bf16]
    aliases = {}
    if kv_t is not None:
        in_specs += [pl.BlockSpec(memory_space=pl.ANY)] * 2
        args += list(kv_t)
        aliases = {3: 3, 4: 4}
    return pl.pallas_call(
        _in_proj_kv_kernel,
        grid=(m // bm,),
        in_specs=in_specs,
        out_specs=[rows(MIX_WIDTH), rows(MIX_WIDTH), rows(MEM_WIDTH), t_spec, t_spec],
        out_shape=[jax.ShapeDtypeStruct((m, MIX_WIDTH), F32), jax.ShapeDtypeStruct((m, MIX_WIDTH), F32),
                   jax.ShapeDtypeStruct((m, MEM_WIDTH), F32), t_shape, t_shape],
        input_output_aliases=aliases,
        compiler_params=_cparams(1),
        name="in_proj_kv",
    )(*args)


def _mem_attend(q, kt, vt):
    lane_head = _iota(q.shape, 1) // HEAD_DIM
    out = jnp.zeros(q.shape, F32)
    for h in range(N_MEM_HEADS):
        qh = jnp.where(lane_head == h, q * ATTN_SCALE, 0.0).astype(BF16)
        s = _dot(qh, kt)
        p = jnp.exp(s - jnp.max(s, axis=-1, keepdims=True))
        l = jnp.sum(p, axis=-1, keepdims=True)
        out = jnp.where(lane_head == h, _dot_nt(p.astype(BF16), vt) / l, out)
    return out


def _out_ffn_kernel(x_ref, mix_ref, mem_ref, *refs, ff_chunk, final, fuse_mem):
    if fuse_mem:
        kt_ref, vt_ref = refs[:2]
        refs = refs[2:]
        mem = _mem_attend(mem_ref[...], kt_ref[0].astype(BF16), vt_ref[0].astype(BF16))
    else:
        mem = mem_ref[...]
    wo_mix_ref, wo_mem_ref, g_ref, wup_ref, wdn_ref, gfin_ref, o_ref = refs
    x = (x_ref[...] + _dot(mix_ref[...].astype(BF16), wo_mix_ref[...])
         + _dot(mem.astype(BF16), wo_mem_ref[...]))
    h = _rms(x, g_ref[...]).astype(BF16)
    o_ref[...] = x
    d_ff = wup_ref.shape[1]
    for c in range(d_ff // ff_chunk):
        u = _dot(h, wup_ref[:, c * ff_chunk:(c + 1) * ff_chunk])
        a = jnp.square(jnp.maximum(u, 0.0)).astype(BF16)
        o_ref[...] += _dot(a, wdn_ref[c * ff_chunk:(c + 1) * ff_chunk, :])
    if final:
        o_ref[...] = _rms(o_ref[...], gfin_ref[...])


def _out_ffn(x, mix, mem, wo_bf16, g, wup_bf16, wdn_bf16, g_final, final, bm, mem_kv=None):
    m, d = x.shape
    wmix = mix.shape[1]
    wmem = mem.shape[1]
    d_ff = wup_bf16.shape[1]
    const = lambda i: (0, 0)
    single = pl.Buffered(1)
    kv_specs, kv_args = [], []
    if mem_kv is not None:
        kt_all, vt_all, layer, batch = mem_kv
        tiles = m // batch // bm
        assert m % batch == 0 and (m // batch) % bm == 0
        kv_spec = pl.BlockSpec((None, 1, wmem, kt_all.shape[-1]), lambda i: (layer, i // tiles, 0, 0))
        kv_specs, kv_args = [kv_spec, kv_spec], [kt_all, vt_all]
    return pl.pallas_call(
        functools.partial(_out_ffn_kernel, ff_chunk=1024, final=final, fuse_mem=mem_kv is not None),
        grid=(m // bm,),
        in_specs=[pl.BlockSpec((bm, d), lambda i: (i, 0)),
                  pl.BlockSpec((bm, wmix), lambda i: (i, 0)),
                  pl.BlockSpec((bm, wmem), lambda i: (i, 0))] + kv_specs + [
                  pl.BlockSpec((wmix, d), const, pipeline_mode=single),
                  pl.BlockSpec((wmem, d), const, pipeline_mode=single),
                  pl.BlockSpec((1, d), const),
                  pl.BlockSpec((d, d_ff), const, pipeline_mode=single),
                  pl.BlockSpec((d_ff, d), const, pipeline_mode=single),
                  pl.BlockSpec((1, d), const)],
        out_specs=pl.BlockSpec((bm, d), lambda i: (i, 0)),
        out_shape=jax.ShapeDtypeStruct((m, d), F32),
        compiler_params=_cparams(1),
        name="out_ffn",
    )(x, mix, mem, *kv_args, wo_bf16[:wmix], wo_bf16[wmix:], g.reshape(1, d), wup_bf16, wdn_bf16,
      g_final.reshape(1, d))


def _mem_kv_kernel(mem_ref, g_ref, w_ref, kt_ref, vt_ref):
    h = _rms(mem_ref[0], g_ref[0]).astype(BF16)
    zt = _dot(h, w_ref[0]).T
    kt_ref[0, 0] = zt[:MEM_WIDTH]
    vt_ref[0, 0] = zt[MEM_WIDTH:]


def _mem_kv(mem, g_mem, w_bf16):
    b, n_mem, d = mem.shape
    depth = w_bf16.shape[0]
    out = jax.ShapeDtypeStruct((depth, b, MEM_WIDTH, n_mem), F32)
    return pl.pallas_call(
        _mem_kv_kernel,
        grid=(depth, b),
        in_specs=[pl.BlockSpec((1, n_mem, d), lambda l, i: (i, 0, 0)),
                  pl.BlockSpec((1, 1, d), lambda l, i: (l, 0, 0)),
                  pl.BlockSpec((1, d, 2 * MEM_WIDTH), lambda l, i: (l, 0, 0))],
        out_specs=[pl.BlockSpec((1, 1, MEM_WIDTH, n_mem), lambda l, i: (l, i, 0, 0))] * 2,
        out_shape=[out, out],
        compiler_params=_cparams(2),
        name="mem_kv",
    )(mem, g_mem.reshape(depth, 1, d), w_bf16)


def _mem_attn_kernel(q_ref, kt_ref, vt_ref, o_ref):
    for i in range(q_ref.shape[0]):
        q = q_ref[i]
        kt = kt_ref[i].astype(BF16)
        vt = vt_ref[i].astype(BF16)
        lane_head = _iota(q.shape, 1) // HEAD_DIM
        out = jnp.zeros(q.shape, F32)
        for h in range(N_MEM_HEADS):
            qh = jnp.where(lane_head == h, q * ATTN_SCALE, 0.0).astype(BF16)
            s = _dot(qh, kt)
            p = jnp.exp(s - jnp.max(s, axis=-1, keepdims=True))
            l = jnp.sum(p, axis=-1, keepdims=True)
            o = _dot_nt(p.astype(BF16), vt)
            out = jnp.where(lane_head == h, o / l, out)
        o_ref[i] = out


def _mem_attn(q3, kt_all, vt_all, layer, tq, bb=1):
    nb, s, w = q3.shape
    n_mem = kt_all.shape[-1]
    assert nb % bb == 0 and s % tq == 0
    kv_spec = pl.BlockSpec((None, bb, w, n_mem), lambda b, i: (layer, b, 0, 0))
    return pl.pallas_call(
        _mem_attn_kernel,
        grid=(nb // bb, s // tq),
        in_specs=[pl.BlockSpec((bb, tq, w), lambda b, i: (b, i, 0)), kv_spec, kv_spec],
        out_specs=pl.BlockSpec((bb, tq, w), lambda b, i: (b, i, 0)),
        out_shape=jax.ShapeDtypeStruct((nb, s, w), F32),
        compiler_params=_cparams(2),
        name="mem_attn",
    )(q3, kt_all, vt_all)


LOG2E = 1.4426950408889634
Q_SCALE_LOG2 = ATTN_SCALE * LOG2E


def _attend(qb, kb, vt, ok):
    s = jnp.where(ok, _dot_nt(kb, qb), NEG)
    m = jnp.max(s, axis=0, keepdims=True)
    p = jnp.exp2(s - m)
    l = jnp.sum(p, axis=0, keepdims=True)
    acc = _dot(vt, p.astype(BF16))
    return m, l, acc


def _moba_prompt_kernel(q_ref, k_ref, vt_in_ref, o_ref, kmean_ref, kb_ref, vt_ref):
    qi = pl.program_id(2)
    blk = MOBA_BLOCK
    nb = k_ref.shape[1] // blk

    @pl.when(qi == 0)
    def _():
        for n in range(nb):
            rows = slice(n * blk, (n + 1) * blk)
            kblk = k_ref[0, rows, :]
            kmean_ref[n:n + 1, :] = jnp.mean(kblk, axis=0, keepdims=True)
            kb_ref[rows, :] = kblk.astype(BF16)
        vt_ref[...] = vt_in_ref[0].astype(BF16)

    q = q_ref[0]
    lane = _iota((blk, LANES), 1)
    lane_nb = _iota((nb, LANES), 1)
    blk_id = _iota((nb, blk), 0)
    kmean = kmean_ref[...]
    valid = blk_id < qi
    qbs, sels = [], []
    for e in range(2):
        qe = jnp.where(lane // HEAD_DIM == e, q, 0.0)
        kme = jnp.where(lane_nb // HEAD_DIM == e, kmean, 0.0)
        gate = jnp.where(valid, _dot_nt(kme, qe, precision=HIGHEST), NEG)
        cnt = jnp.zeros((nb, blk), F32)
        for m in range(nb):
            gm = gate[m:m + 1, :]
            beats = (gm > gate) | ((gm == gate) & (blk_id > m))
            cnt = cnt + jnp.where(beats, 1.0, 0.0)
        sels.append(jnp.where((cnt < MOBA_TOPK) & valid, 1.0, 0.0))
        qbs.append((qe * Q_SCALE_LOG2).astype(BF16))

    qb2 = jnp.concatenate(qbs, axis=0)
    sel2 = jnp.concatenate(sels, axis=1)
    causal = _iota((blk, blk), 0) <= _iota((blk, blk), 1)
    causal2 = jnp.where(jnp.concatenate([causal, causal], axis=1), 1.0, 0.0)
    feat = _iota((LANES, blk), 0)

    for n_past in range(nb):
        @pl.when(qi == n_past)
        def _(n_past=n_past):
            n_keys = (n_past + 1) * blk
            ok = jnp.concatenate([jnp.broadcast_to(sel2[n:n + 1, :], (blk, 2 * blk))
                                  for n in range(n_past)] + [causal2], axis=0) > 0.5
            _, l, acc = _attend(qb2, kb_ref[0:n_keys, :], vt_ref[:, 0:n_keys], ok)
            out = acc / l
            o_ref[0] = jnp.where(feat < HEAD_DIM, out[:, :blk], out[:, blk:]).T


def _moba_prompt(q3, k3, vt_all, layer):
    b, s, w = q3.shape
    blk = MOBA_BLOCK
    nb = s // blk
    assert s % blk == 0 and w % LANES == 0 and vt_all.shape[1:] == (b, w, s)
    k_spec = pl.BlockSpec((1, s, LANES), lambda i, p, j: (i, 0, p))
    vt_spec = pl.BlockSpec((None, 1, LANES, s), lambda i, p, j: (layer, i, p, 0))
    return pl.pallas_call(
        _moba_prompt_kernel,
        grid=(b, w // LANES, nb),
        in_specs=[pl.BlockSpec((1, blk, LANES), lambda i, p, j: (i, j, p)), k_spec, vt_spec],
        out_specs=pl.BlockSpec((1, blk, LANES), lambda i, p, j: (i, j, p)),
        out_shape=jax.ShapeDtypeStruct((b, s, w), F32),
        scratch_shapes=[pltpu.VMEM((nb, LANES), F32),
                        pltpu.VMEM((s, LANES), BF16),
                        pltpu.VMEM((LANES, s), BF16)],
        compiler_params=_cparams(3),
        name="moba_prompt",
    )(q3, k3, vt_all)


def _block_diag_q(q, n_heads, rows_per_tok):
    t_len, w = q.shape
    hrow = _iota((rows_per_tok, w), 0)
    lane_head = _iota((rows_per_tok, w), 1) // HEAD_DIM
    keep = (hrow == lane_head) & (hrow < n_heads)
    parts = [jnp.where(keep, jnp.broadcast_to(q[t:t + 1, :], (rows_per_tok, w)), 0.0)
             for t in range(t_len)]
    return jnp.concatenate(parts, axis=0)


def _head_diag_rows(o, t_len, rows_per_tok):
    w = o.shape[1]
    hrow = _iota((rows_per_tok, w), 0)
    lane_head = _iota((rows_per_tok, w), 1) // HEAD_DIM
    keep = hrow == lane_head
    return [jnp.sum(jnp.where(keep, o[t * rows_per_tok:(t + 1) * rows_per_tok, :], 0.0), axis=0,
                    keepdims=True) for t in range(t_len)]


MOBA_QROWS = 16
MOBA_CHUNK = 8
MOBA_AHEAD = 3
MOBA_SLOTS = MOBA_AHEAD + 1


def _moba_sample_kernel(pt_ref, q_ref, kn_ref, vn_ref, kt_hbm, vt_hbm, o_ref,
                        buf_ref, sem, sc_ref, ksum_ref, acc_ref, *, layer, n_blocks):
    b = pl.program_id(0)
    n_batch = pl.num_programs(0)
    t_len = q_ref.shape[1]
    nr = t_len * MOBA_QROWS
    chunk = MOBA_CHUNK
    k_chunks = 2 * n_blocks // chunk
    per_b = 2 * k_chunks
    total = n_batch * per_b

    def page_copy(src_hbm, bb, page, slot, j):
        return pltpu.make_async_copy(src_hbm.at[layer, pt_ref[bb, page]], buf_ref.at[slot, j],
                                     sem.at[slot])

    def start_chunk(c):
        bb = c // per_b
        i = c % per_b
        slot = c % MOBA_SLOTS

        @pl.when(i < k_chunks)
        def _():
            for j in range(chunk):
                page_copy(kt_hbm, bb, i * chunk + j, slot, j).start()

        @pl.when(i >= k_chunks)
        def _():
            for j in range(chunk):
                page_copy(vt_hbm, bb, (i - k_chunks) * chunk + j, slot, j).start()

    def next_chunk(c):
        @pl.when(c + MOBA_AHEAD < total)
        def _():
            start_chunk(c + MOBA_AHEAD)

        slot = c % MOBA_SLOTS
        for j in range(chunk):
            page_copy(kt_hbm, 0, 0, slot, j).wait()
        return slot

    @pl.when(b == 0)
    def _():
        for c in range(MOBA_AHEAD):
            start_chunk(c)

    qbd = _block_diag_q(q_ref[0], N_MIX_HEADS, MOBA_QROWS)
    qb = (qbd * ATTN_SCALE).astype(BF16)
    lane_w = _iota(ksum_ref.shape, 1)

    def k_body(i, carry):
        slot = next_chunk(b * per_b + i)
        for jj in range(chunk // 2):
            ka = buf_ref[slot, 2 * jj]
            kb = buf_ref[slot, 2 * jj + 1]
            page = i * chunk + 2 * jj
            sc_ref[page] = _dot(qb, ka.astype(BF16))
            sc_ref[page + 1] = _dot(qb, kb.astype(BF16))
            col = jnp.sum(ka + kb, axis=1, keepdims=True)
            ksum_ref[...] = jnp.where(lane_w == page // 2, col, ksum_ref[...])
        return carry

    ksum_ref[...] = jnp.zeros(ksum_ref.shape, F32)
    lax.fori_loop(0, k_chunks, k_body, 0)

    lane = _iota((nr, LANES), 1)
    gmat = _dot(qbd, ksum_ref[...], precision=HIGHEST) * (1.0 / MOBA_BLOCK)
    gmat = jnp.where(lane < n_blocks, gmat, NEG)
    cnt = jnp.zeros((nr, LANES), F32)
    for m in range(n_blocks):
        gm = jnp.broadcast_to(gmat[:, m:m + 1], (nr, LANES))
        beats = (gm > gmat) | ((gm == gmat) & (lane > m))
        cnt = cnt + jnp.where(beats, 1.0, 0.0)
    sel = jnp.where((cnt < MOBA_TOPK) & (lane < n_blocks), 1.0, 0.0)

    def block_ok(n):
        return jnp.broadcast_to(sel[:, n:n + 1], (nr, LANES)) > 0.5

    trow = _iota((nr, 1), 0) // MOBA_QROWS
    s_own = []
    for u in range(t_len):
        su = jnp.sum(qbd * kn_ref[0, u:u + 1, :], axis=1, keepdims=True) * ATTN_SCALE
        s_own.append(jnp.where(trow >= u, su, NEG))

    mx = jnp.full((nr, LANES), NEG, F32)
    for n in range(n_blocks):
        ok = block_ok(n)
        for half in range(2):
            mx = jnp.maximum(mx, jnp.where(ok, sc_ref[2 * n + half], NEG))
    m = jnp.max(mx, axis=1, keepdims=True)
    for su in s_own:
        m = jnp.maximum(m, su)
    lsum = jnp.zeros((nr, LANES), F32)
    for n in range(n_blocks):
        ok = block_ok(n)
        for half in range(2):
            p = jnp.exp(jnp.where(ok, sc_ref[2 * n + half], NEG) - m)
            sc_ref[2 * n + half] = p
            lsum = lsum + p
    l = jnp.sum(lsum, axis=1, keepdims=True)
    acc = jnp.zeros(acc_ref.shape, F32)
    for u in range(t_len):
        pu = jnp.exp(s_own[u] - m)
        l = l + pu
        acc = acc + pu * vn_ref[0, u:u + 1, :]
    acc_ref[...] = acc

    def v_body(i, carry):
        slot = next_chunk(b * per_b + k_chunks + i)
        for j in range(chunk):
            p = sc_ref[i * chunk + j].astype(BF16)
            acc_ref[...] += _dot_nt(p, buf_ref[slot, j].astype(BF16))
        return carry

    lax.fori_loop(0, k_chunks, v_body, 0)

    o = acc_ref[...] * (1.0 / l)
    for t, r in enumerate(_head_diag_rows(o, t_len, MOBA_QROWS)):
        o_ref[0, t:t + 1, :] = r


def _moba_sample(q3, kn3, vn3, ckt, cvt, layer, page_table):
    bd, t_len, w = q3.shape
    n_pages = page_table.shape[1]
    pages_per_block = MOBA_BLOCK // PAGE_SIZE
    assert pages_per_block == 2 and n_pages % pages_per_block == 0 and t_len <= MOBA_BLOCK
    nb = n_pages // pages_per_block
    assert n_pages % MOBA_CHUNK == 0 and MOBA_CHUNK % pages_per_block == 0 and nb <= LANES
    nr = t_len * MOBA_QROWS
    tok = pl.BlockSpec((1, t_len, w), lambda b, pt: (b, 0, 0))
    hbm = pl.BlockSpec(memory_space=pl.ANY)
    grid_spec = pltpu.PrefetchScalarGridSpec(
        num_scalar_prefetch=1,
        grid=(bd,),
        in_specs=[tok, tok, tok, hbm, hbm],
        out_specs=tok,
        scratch_shapes=[pltpu.VMEM((MOBA_SLOTS, MOBA_CHUNK, w, PAGE_SIZE), F32),
                        pltpu.SemaphoreType.DMA((MOBA_SLOTS,)),
                        pltpu.VMEM((n_pages, nr, PAGE_SIZE), F32),
                        pltpu.VMEM((w, LANES), F32),
                        pltpu.VMEM((nr, w), F32)])
    return pl.pallas_call(
        functools.partial(_moba_sample_kernel, layer=layer, n_blocks=nb),
        grid_spec=grid_spec,
        out_shape=jax.ShapeDtypeStruct((bd, t_len, w), F32),
        compiler_params=_cparams(1),
        name="moba_sample",
    )(page_table, q3, kn3, vn3, ckt, cvt)


def _dil_prompt_kernel(q0_ref, q1_ref, q2_ref, k0_ref, k1_ref, k2_ref, v0_ref, v1_ref, v2_ref,
                       o_ref, kb_ref, vt_ref, og_ref, lse_ref):
    qi = pl.program_id(2)
    tq = q0_ref.shape[1]
    s_len = k0_ref.shape[1]
    q_refs = (q0_ref, q1_ref, q2_ref)

    @pl.when(qi == 0)
    def _():
        for g, (k_ref, v_ref) in enumerate(((k0_ref, v0_ref), (k1_ref, v1_ref), (k2_ref, v2_ref))):
            kb_ref[g] = k_ref[0].astype(BF16)
            for n in range(s_len // LANES):
                vt_ref[g, n] = v_ref[0, :, n * LANES:(n + 1) * LANES].astype(BF16)

    lane = _iota((tq, LANES), 1)
    t0 = qi * tq

    def group(g, k0, n_keys, dist0):
        window, dil = SWA_GROUPS[g]
        d = dist0 + _iota((n_keys, tq), 1) - _iota((n_keys, tq), 0)
        ok = (d >= 0) & (d <= window)
        if dil > 1:
            ok = ok & ((d & (dil - 1)) == 0)
        okf = jnp.where(ok, 1.0, 0.0)
        ok2 = jnp.concatenate([okf, okf], axis=1) > 0.5
        kb = kb_ref[g, pl.ds(k0, n_keys), :]
        vt = jnp.concatenate([vt_ref[g, k0 // LANES + i] for i in range(n_keys // LANES)], axis=1)
        qb2 = jnp.concatenate(
            [jnp.where(lane // HEAD_DIM == e, q_refs[g][0] * Q_SCALE_LOG2, 0.0).astype(BF16)
             for e in range(2)], axis=0)
        m, l, acc = _attend(qb2, kb, vt, ok2)
        return acc / l, m + jnp.log2(l)

    os = [[], []]
    lses = [[], []]
    for g, (window, dil) in enumerate(SWA_GROUPS):
        if window + tq < s_len:
            n_keys = window + tq
            k0 = pl.multiple_of(jnp.maximum(t0 + tq - n_keys, 0), LANES)
            o2, lse2 = group(g, k0, n_keys, t0 - k0)
        else:
            for qv in range(s_len // tq):
                @pl.when(qi == qv)
                def _(qv=qv, g=g, window=window):
                    end = (qv + 1) * tq
                    n_keys = min(window + tq, end)
                    o2, lse2 = group(g, end - n_keys, n_keys, qv * tq - (end - n_keys))
                    og_ref[...] = o2
                    lse_ref[0:1, :] = lse2
            o2 = og_ref[...]
            lse2 = lse_ref[0:1, :]
        for e in range(2):
            os[e].append(o2[:, e * tq:(e + 1) * tq])
            lses[e].append(lse2[:, e * tq:(e + 1) * tq])
    outs = []
    for e in range(2):
        mx = jnp.maximum(jnp.maximum(lses[e][0], lses[e][1]), lses[e][2])
        ws = [jnp.exp2(x - mx) for x in lses[e]]
        tot = ws[0] + ws[1] + ws[2]
        outs.append((ws[0] * os[e][0] + ws[1] * os[e][1] + ws[2] * os[e][2]) / tot)
    feat = _iota((LANES, tq), 0)
    o_ref[0] = jnp.where(feat < HEAD_DIM, outs[0], outs[1]).T


def _dil_prompt(q3, k3, vt_all, layer, tq):
    b, s, w = q3.shape
    n_pairs = SWA_WIDTH // LANES
    assert w == len(SWA_GROUPS) * SWA_WIDTH and s % tq == 0 and vt_all.shape[1:] == (b, w, s)
    assert all(d & (d - 1) == 0 and wd % LANES == 0 for wd, d in SWA_GROUPS) and tq % LANES == 0

    def q_spec(g):
        return pl.BlockSpec((1, tq, LANES), lambda i, p, j: (i, j, g * n_pairs + p))

    def kv_spec(g):
        return pl.BlockSpec((1, s, LANES), lambda i, p, j: (i, 0, g * n_pairs + p))

    def vt_spec(g):
        return pl.BlockSpec((None, 1, LANES, s), lambda i, p, j: (layer, i, g * n_pairs + p, 0))

    n_g = len(SWA_GROUPS)
    return pl.pallas_call(
        _dil_prompt_kernel,
        grid=(b, n_pairs, s // tq),
        in_specs=([q_spec(g) for g in range(n_g)] + [kv_spec(g) for g in range(n_g)]
                  + [vt_spec(g) for g in range(n_g)]),
        out_specs=pl.BlockSpec((1, tq, LANES), lambda i, p, j: (i, j, p)),
        out_shape=jax.ShapeDtypeStruct((b, s, SWA_WIDTH), F32),
        scratch_shapes=[pltpu.VMEM((n_g, s, LANES), BF16),
                        pltpu.VMEM((n_g, s // LANES, LANES, LANES), BF16),
                        pltpu.VMEM((LANES, 2 * tq), F32),
                        pltpu.VMEM((SUBLANES, 2 * tq), F32)],
        compiler_params=_cparams(3),
        name="dil_prompt",
    )(q3, q3, q3, k3, k3, k3, vt_all, vt_all, vt_all)


SWA_QROWS = 8


def _dil_sample_kernel(q_ref, kn_ref, vn_ref, b0_ref, b1_ref, b2_ref, o_ref):
    t_len = q_ref.shape[1]
    nr = t_len * SWA_QROWS
    q = q_ref[0]
    kn = kn_ref[0]
    vn = vn_ref[0]
    trow = _iota((nr, 1), 0) // SWA_QROWS
    os, lses = [], []
    for g, ((window, dil), buf_ref) in enumerate(zip(SWA_GROUPS, (b0_ref, b1_ref, b2_ref))):
        sl = slice(g * SWA_WIDTH, (g + 1) * SWA_WIDTH)
        qbd = _block_diag_q(q[:, sl], SWA_HEADS, SWA_QROWS)
        kt = buf_ref[0, 0].astype(BF16)
        vt = buf_ref[0, 1].astype(BF16)
        s = _dot((qbd * ATTN_SCALE).astype(BF16), kt)
        d = window + (_iota((nr, window), 0) // SWA_QROWS) - _iota((nr, window), 1)
        s = jnp.where((d <= window) & ((d & (dil - 1)) == 0), s, NEG)
        s_new = []
        for u in range(t_len):
            su = jnp.sum(qbd * kn[u:u + 1, sl], axis=1, keepdims=True) * ATTN_SCALE
            du = trow - u
            s_new.append(jnp.where((du >= 0) & ((du & (dil - 1)) == 0), su, NEG))
        m = jnp.max(s, axis=1, keepdims=True)
        for su in s_new:
            m = jnp.maximum(m, su)
        p = jnp.exp(s - m)
        l = jnp.sum(p, axis=1, keepdims=True)
        o = _dot_nt(p.astype(BF16), vt)
        for u in range(t_len):
            pu = jnp.exp(s_new[u] - m)
            l = l + pu
            o = o + pu * vn[u:u + 1, sl]
        os.append(o / l)
        lses.append(m + jnp.log(l))
    mx = jnp.maximum(jnp.maximum(lses[0], lses[1]), lses[2])
    ws = [jnp.exp(x - mx) for x in lses]
    tot = ws[0] + ws[1] + ws[2]
    out = (ws[0] * os[0] + ws[1] * os[1] + ws[2] * os[2]) / tot
    for t, r in enumerate(_head_diag_rows(out, t_len, SWA_QROWS)):
        o_ref[0, t:t + 1, :] = r


def _dil_sample(q3, kn3, vn3, bufs_t, layer):
    bd, t_len, w = q3.shape
    tok = pl.BlockSpec((1, t_len, w), lambda b: (b, 0, 0))
    buf_specs = []
    for (window, _), buf in zip(SWA_GROUPS, bufs_t):
        assert buf.shape[1:] == (bd, 2, SWA_WIDTH, window)
        buf_specs.append(pl.BlockSpec((None, 1, 2, SWA_WIDTH, window),
                                      lambda b: (layer, b, 0, 0, 0)))
    return pl.pallas_call(
        _dil_sample_kernel,
        grid=(bd,),
        in_specs=[tok, tok, tok] + buf_specs,
        out_specs=pl.BlockSpec((1, t_len, SWA_WIDTH), lambda b: (b, 0, 0)),
        out_shape=jax.ShapeDtypeStruct((bd, t_len, SWA_WIDTH), F32),
        compiler_params=_cparams(1),
        name="dil_sample",
    )(q3, kn3, vn3, *bufs_t)


def _ln_swish(y, lg, lb):
    mu = jnp.mean(y, axis=-1, keepdims=True)
    yc = y - mu
    var = jnp.mean(yc * yc, axis=-1, keepdims=True)
    z = yc * lax.rsqrt(var + LN_EPS) * lg + lb
    return z * _sigmoid(z)


def _conv_prompt_kernel(a_ref, g_ref, w_ref, b_ref, lg_ref, lb_ref, o_ref, tail_ref, ctx_ref,
                        sh_ref):
    qi = pl.program_id(1)
    tq = a_ref.shape[1]
    halo = CONV_HALO
    sub = SUBLANES

    @pl.when(qi == 0)
    def _():
        ctx_ref[0:halo, :] = jnp.zeros((halo, ctx_ref.shape[1]), F32)

    ctx_ref[halo:halo + tq, :] = a_ref[0] * _sigmoid(g_ref[0])
    n_sh = sh_ref.shape[1]
    for r in range(1, sub):
        sh_ref[r - 1] = ctx_ref[r:r + n_sh, :]
    first = halo - (CONV_WIDTH - 1)
    acc = jnp.zeros((tq, ctx_ref.shape[1]), F32)
    for j in range(CONV_WIDTH):
        start, phase = (first + j) // sub * sub, (first + j) % sub
        src = ctx_ref[start:start + tq, :] if phase == 0 else sh_ref[phase - 1, start:start + tq, :]
        acc = acc + src * w_ref[j:j + 1, :]
    o_ref[0] = _ln_swish(acc + b_ref[...], lg_ref[...], lb_ref[...])
    tail = ctx_ref[tq:tq + halo, :]
    ctx_ref[0:halo, :] = tail
    tail_ref[0] = tail


def _conv_prompt(a3, g3, w, b, lg, lb, tq):
    bsz, s, c = a3.shape
    assert s % tq == 0 and tq >= CONV_HALO and CONV_HALO % SUBLANES == 0
    tile = pl.BlockSpec((1, tq, c), lambda i, j: (i, j, 0))
    vec = pl.BlockSpec((1, c), lambda i, j: (0, 0))
    return pl.pallas_call(
        _conv_prompt_kernel,
        grid=(bsz, s // tq),
        in_specs=[tile, tile, pl.BlockSpec((CONV_WIDTH, c), lambda i, j: (0, 0)), vec, vec, vec],
        out_specs=[tile, pl.BlockSpec((1, CONV_HALO, c), lambda i, j: (i, 0, 0))],
        out_shape=[jax.ShapeDtypeStruct((bsz, s, c), F32),
                   jax.ShapeDtypeStruct((bsz, CONV_HALO, c), F32)],
        scratch_shapes=[pltpu.VMEM((CONV_HALO + tq, c), F32),
                        pltpu.VMEM((SUBLANES - 1, CONV_HALO - SUBLANES + tq, c), F32)],
        compiler_params=_cparams(2),
        name="conv_prompt",
    )(a3, g3, w, b.reshape(1, c), lg.reshape(1, c), lb.reshape(1, c))


def _conv_sample_kernel(st_ref, a_ref, g_ref, w_ref, b_ref, lg_ref, lb_ref, o_ref, ns_ref):
    n_hist = st_ref.shape[0]
    t_len = a_ref.shape[0]
    u = [a_ref[t] * _sigmoid(g_ref[t]) for t in range(t_len)]

    def ctx(i):
        return st_ref[i] if i < n_hist else u[i - n_hist]

    for t in range(t_len):
        acc = jnp.zeros(u[0].shape, F32)
        for j in range(CONV_WIDTH):
            acc = acc + ctx(t + j) * w_ref[j:j + 1, :]
        o_ref[t] = _ln_swish(acc + b_ref[...], lg_ref[...], lb_ref[...])
    for i in range(n_hist):
        ns_ref[i] = ctx(i + t_len)


def _conv_sample(state_t, a_t, g_t, w, b, lg, lb, layer):
    _, n_hist, bd, c = state_t.shape
    t_len = a_t.shape[0]
    assert n_hist == CONV_WIDTH - 1
    tok = pl.BlockSpec((t_len, bd, c), lambda i: (0, 0, 0))
    vec = pl.BlockSpec((1, c), lambda i: (0, 0))
    return pl.pallas_call(
        _conv_sample_kernel,
        grid=(1,),
        in_specs=[pl.BlockSpec((None, n_hist, bd, c), lambda i: (layer, 0, 0, 0)), tok, tok,
                  pl.BlockSpec((CONV_WIDTH, c), lambda i: (0, 0)), vec, vec, vec],
        out_specs=[tok, pl.BlockSpec((n_hist, bd, c), lambda i: (0, 0, 0))],
        out_shape=[jax.ShapeDtypeStruct((t_len, bd, c), F32),
                   jax.ShapeDtypeStruct((n_hist, bd, c), F32)],
        compiler_params=_cparams(1),
        name="conv_sample",
    )(state_t, a_t, g_t, w, b.reshape(1, c), lg.reshape(1, c), lb.reshape(1, c))


def kernel(x_prompt, x_sample, mem_prompt, cache_moba_k, cache_moba_v, page_table, state_swa_w128, state_swa_w512, state_swa_w2048, state_conv, cache_mem_k, cache_mem_v, g_mix, g_mem, w_mem_kv, w_in_a, w_out_a, w_in_b, w_out_b, w_in_c, conv_w, conv_b, conv_ln_g, conv_ln_b, w_out_c, g_ffn, w_ffn_up, w_ffn_down, g_final):
    b, s, d = x_prompt.shape
    bd, t_len, _ = x_sample.shape
    depth = g_mix.shape[0]
    kinds = tuple(i % N_MIXERS for i in range(depth))
    past_len = page_table.shape[1] * PAGE_SIZE
    n_mem = mem_prompt.shape[1]
    bm = 512
    bf = lambda w: w.astype(BF16)

    xp = x_prompt.reshape(b * s, d)
    xs = x_sample.reshape(bd * t_len, d)

    mem_kt, mem_vt = _mem_kv(mem_prompt, g_mem, bf(w_mem_kv))
    cmem_kt = cache_mem_k.transpose(0, 1, 3, 4, 2).reshape(depth, bd, MEM_WIDTH, n_mem)
    cmem_vt = cache_mem_v.transpose(0, 1, 3, 4, 2).reshape(depth, bd, MEM_WIDTH, n_mem)
    n_pool = cache_moba_k.shape[1]
    ckt = cache_moba_k.transpose(0, 1, 3, 4, 2).reshape(-1, n_pool, MIX_WIDTH, PAGE_SIZE)
    cvt = cache_moba_v.transpose(0, 1, 3, 4, 2).reshape(-1, n_pool, MIX_WIDTH, PAGE_SIZE)
    swa_states = (state_swa_w128, state_swa_w512, state_swa_w2048)
    swa_t = [st.transpose(0, 1, 3, 4, 5, 2).reshape(st.shape[0], bd, 2, SWA_WIDTH, st.shape[2])
             for st in swa_states]
    conv_t = state_conv.transpose(0, 2, 1, 3)

    moba_ks, moba_vs = [], []
    kv_t = {}
    for kind in (0, 1):
        n_kind = kinds.count(kind)
        zeros = jnp.zeros((n_kind, b, MIX_WIDTH, s), F32) if n_kind > 1 else None
        kv_t[kind] = None if zeros is None else (zeros, zeros)
    swa_s = [[] for _ in SWA_GROUPS]
    conv_p, conv_s = [], []

    for l in range(depth):
        kind = kinds[l]
        j = kinds[:l].count(kind)
        if kind == 2:
            w_in = bf(w_in_c[j])
            splits = (MIX_WIDTH, MIX_WIDTH, MEM_WIDTH)
            ap, gp, mqp = _in_proj(xp, g_mix[l], w_in, splits, bm)
            a_s, g_s, mqs = _in_proj(xs, g_mix[l], w_in, splits, bd * t_len)
            mix_p, tail = _conv_prompt(ap.reshape(b, s, -1), gp.reshape(b, s, -1), conv_w[j],
                                       conv_b[j], conv_ln_g[j], conv_ln_b[j], 256)
            mix_p = mix_p.reshape(b * s, -1)
            tb = lambda z: z.reshape(bd, t_len, -1).transpose(1, 0, 2)
            mix_s, new_state = _conv_sample(conv_t, tb(a_s), tb(g_s), conv_w[j], conv_b[j],
                                            conv_ln_g[j], conv_ln_b[j], j)
            mix_s = mix_s.transpose(1, 0, 2).reshape(bd * t_len, -1)
            conv_p.append(tail[:, CONV_HALO - (CONV_WIDTH - 1):])
            conv_s.append(new_state.transpose(1, 0, 2))
            w_out = bf(w_out_c[j])
        else:
            w_in = bf(w_in_a[j] if kind == 0 else w_in_b[j])
            splits = (MIX_WIDTH, MIX_WIDTH, MIX_WIDTH, MEM_WIDTH)
            qp, kp, mqp, kt_all, vt_all = _in_proj_kv(xp, g_mix[l], w_in, kv_t[kind], j,
                                                     kinds.count(kind), b, bm)
            kv_t[kind] = (kt_all, vt_all)
            qs, ks, vs, mqs = _in_proj(xs, g_mix[l], w_in, splits, bd * t_len)
            qp3, kp3 = qp.reshape(b, s, MIX_WIDTH), kp.reshape(b, s, MIX_WIDTH)
            qs3, ks3, vs3 = (z.reshape(bd, t_len, MIX_WIDTH) for z in (qs, ks, vs))
            if kind == 0:
                mix_p = _moba_prompt(qp3, kp3, vt_all, j)
                mix_s = _moba_sample(qs3, ks3, vs3, ckt, cvt, j, page_table)
                moba_ks.append(ks.reshape(bd, t_len, N_MIX_HEADS, HEAD_DIM))
                moba_vs.append(vs.reshape(bd, t_len, N_MIX_HEADS, HEAD_DIM))
                w_out = bf(w_out_a[j])
            else:
                mix_p = _dil_prompt(qp3, kp3, vt_all, j, 256)
                mix_s = _dil_sample(qs3, ks3, vs3, swa_t, j)
                for g, (window, _) in enumerate(SWA_GROUPS):
                    sl = slice(g * SWA_WIDTH, (g + 1) * SWA_WIDTH)
                    heads = lambda z: z.reshape(z.shape[0], z.shape[1], SWA_HEADS, HEAD_DIM)
                    kv_s = jnp.stack([heads(ks3[:, :, sl]), heads(vs3[:, :, sl])], axis=2)
                    assert swa_states[g].shape[2] == window
                    full = jnp.concatenate([swa_states[g][j], kv_s], axis=1)
                    swa_s[g].append(full[:, full.shape[1] - min(window, past_len + t_len):])
                w_out = bf(w_out_b[j])
            mix_p = mix_p.reshape(b * s, -1)
            mix_s = mix_s.reshape(bd * t_len, -1)
        mem_s = _mem_attn(mqs.reshape(bd, t_len, MEM_WIDTH), cmem_kt, cmem_vt, l, t_len,
                          bb=8 if bd % 8 == 0 else 1)
        final = l == depth - 1
        wup, wdn = bf(w_ffn_up[l]), bf(w_ffn_down[l])
        xp = _out_ffn(xp, mix_p, mqp, w_out, g_ffn[l], wup, wdn, g_final, final, bm,
                      mem_kv=(mem_kt, mem_vt, l, b))
        xs = _out_ffn(xs, mix_s, mem_s.reshape(bd * t_len, MEM_WIDTH), w_out, g_ffn[l], wup, wdn,
                      g_final, final, bd * t_len)

    kt_a, vt_a = kv_t[0]
    token_major = lambda zt: zt.reshape(zt.shape[0], b, -1, HEAD_DIM, s).transpose(0, 1, 4, 2, 3)
    moba_kp, moba_vp = token_major(kt_a), token_major(vt_a)
    kt_b, vt_b = kv_t[1]
    swa_p = []
    for g, (window, _) in enumerate(SWA_GROUPS):
        sl = slice(g * SWA_WIDTH, (g + 1) * SWA_WIDTH)
        last = s - min(window, s)
        kv = jnp.stack([kt_b[:, :, sl, last:], vt_b[:, :, sl, last:]], axis=2)
        kv = kv.reshape(kv.shape[0], b, 2, SWA_HEADS, HEAD_DIM, s - last)
        swa_p.append(kv.transpose(0, 1, 5, 2, 3, 4))

    heads_t = lambda zt: zt.reshape(depth, b, N_MEM_HEADS, HEAD_DIM, n_mem).transpose(0, 1, 4, 2, 3)
    return (xp.reshape(b, s, d), xs.reshape(bd, t_len, d),
            moba_kp, moba_vp, jnp.stack(moba_ks), jnp.stack(moba_vs),
            swa_p[0], swa_p[1], swa_p[2],
            jnp.stack(swa_s[0]), jnp.stack(swa_s[1]), jnp.stack(swa_s[2]),
            jnp.stack(conv_p), jnp.stack(conv_s),
            heads_t(mem_kt), heads_t(mem_vt))
```
